```python
import jax, jax.numpy as jnp
from jax import lax
import numpy as np

D_MODEL = 1024
BATCH = 32
SEQ = 2048
DEPTH = 4
DEC_BATCH = 32
DEC_SEQ = 32
PAST_LEN = 4096

CHUNK = 64
HEAD_DIM = 64
FOX_HEADS = 8
RWKV_HEADS = 4
RET_HEADS = 4
FOX_WIDTH = FOX_HEADS * HEAD_DIM
RWKV_WIDTH = RWKV_HEADS * HEAD_DIM
RET_WIDTH = RET_HEADS * HEAD_DIM
MIX_WIDTH = FOX_WIDTH + RWKV_WIDTH + RET_WIDTH
FOX_QBLOCK = 128
RWKV_DECAY_RANK = 32
RWKV_A_RANK = 32
RWKV_GATE_RANK = 32
D_FF = 2816
PLE_DIM = 256
ROPE_BASE = 10000.0
NORM_EPS = 1e-6
RWKV_GN_EPS = 64e-5
RET_GN_EPS = 1e-6
FORGET_BIAS_MEAN = 2.0

FOX_SPLITS = (FOX_WIDTH, FOX_WIDTH, FOX_WIDTH, FOX_HEADS, FOX_WIDTH)
RWKV_SPLITS = (RWKV_WIDTH, RWKV_DECAY_RANK, RWKV_WIDTH, RWKV_WIDTH, RWKV_A_RANK, RWKV_GATE_RANK)
RET_SPLITS = (RET_WIDTH, RET_WIDTH, RET_WIDTH, RET_WIDTH)
FOX_COLS = sum(FOX_SPLITS)
RWKV_COLS = sum(RWKV_SPLITS)
RET_COLS = sum(RET_SPLITS)
IN_COLS = FOX_COLS + RWKV_COLS + RET_COLS

kernel_name = 'hybrid_fox_rwkv7_retention_stream'

F32 = jnp.float32


def _split(z, sizes):
    cuts = [int(c) for c in np.cumsum(sizes)[:-1]]
    return jnp.split(z, cuts, axis=-1)


def rmsnorm(x, g):
    xf = x.astype(F32)
    y = xf * lax.rsqrt(jnp.mean(xf * xf, axis=-1, keepdims=True) + NORM_EPS)
    return (y * g.astype(F32)).astype(x.dtype)


def head_norm(x, eps):
    xf = x.astype(F32)
    mu = jnp.mean(xf, axis=-1, keepdims=True)
    var = jnp.mean(jnp.square(xf - mu), axis=-1, keepdims=True)
    return (xf - mu) * lax.rsqrt(var + eps)


def swiglu(h, wg, wu, wd):
    return (jax.nn.silu(h @ wg) * (h @ wu)) @ wd


def rotary(x, pos):
    half = HEAD_DIM // 2
    inv = 1.0 / (ROPE_BASE ** (jnp.arange(half, dtype=F32) / half))
    ang = pos.astype(F32)[:, None] * inv[None, :]
    cos = jnp.cos(ang)[None, :, None, :]
    sin = jnp.sin(ang)[None, :, None, :]
    xf = x.astype(F32)
    x1, x2 = xf[..., :half], xf[..., half:]
    return jnp.concatenate([x1 * cos - x2 * sin, x2 * cos + x1 * sin], axis=-1).astype(x.dtype)


def fox_block(q, k, v, cq, ck, mask):
    s = jnp.einsum('bqhd,bkhd->bhqk', q, k).astype(F32) * (HEAD_DIM ** -0.5)
    s = s + jnp.swapaxes(cq, 1, 2)[:, :, :, None] - jnp.swapaxes(ck, 1, 2)[:, :, None, :]
    s = jnp.where(mask[None, None], s, -jnp.inf)
    p = jax.nn.softmax(s, axis=-1).astype(v.dtype)
    return jnp.einsum('bhqk,bkhd->bqhd', p, v)


def fox_prompt(q, k, v, logf):
    T = q.shape[1]
    c = jnp.cumsum(logf.astype(F32), axis=1)
    outs = []
    for i in range(T // FOX_QBLOCK):
        s0, e = i * FOX_QBLOCK, (i + 1) * FOX_QBLOCK
        mask = jnp.arange(s0, e)[:, None] >= jnp.arange(e)[None, :]
        outs.append(fox_block(q[:, s0:e], k[:, :e], v[:, :e], c[:, s0:e], c[:, :e], mask))
    return jnp.concatenate(outs, axis=1)


def fox_sample(q, k, v, logf, ck_cache, cv_cache, clf_cache):
    P, n = ck_cache.shape[1], q.shape[1]
    k_all = jnp.concatenate([ck_cache.astype(k.dtype), k], axis=1)
    v_all = jnp.concatenate([cv_cache.astype(v.dtype), v], axis=1)
    c = jnp.cumsum(jnp.concatenate([clf_cache.astype(F32), logf.astype(F32)], axis=1), axis=1)
    mask = (P + jnp.arange(n))[:, None] >= jnp.arange(P + n)[None, :]
    return fox_block(q, k_all, v_all, c[:, P:], c, mask)


def rwkv_mix(z, prev, S0, mu, w0, w_up, a0, a_up, g_up, k_k, k_a, r_k, ln_w, ln_b):
    B, T, _ = z.shape
    shifted = jnp.concatenate([prev.astype(z.dtype), z[:, :-1]], axis=1)
    zs = z + mu * (shifted - z)
    r, wd, k, v, ad, gd = _split(zs, RWKV_SPLITS)
    w = -jax.nn.softplus(-(w0 + jnp.tanh(wd) @ w_up).astype(F32)) - 0.5
    decay = jnp.exp(-jnp.exp(w))
    a = jax.nn.sigmoid((a0 + ad @ a_up).astype(F32))
    g = (jax.nn.sigmoid(gd) @ g_up).astype(F32)
    hd = lambda t: t.astype(F32).reshape(B, T, RWKV_HEADS, HEAD_DIM)
    r, k, v, a, decay = hd(r), hd(k), hd(v), hd(a), hd(decay)
    kk = k * k_k.astype(F32).reshape(RWKV_HEADS, HEAD_DIM)
    kk = kk * lax.rsqrt(jnp.maximum(jnp.sum(kk * kk, axis=-1, keepdims=True), 1e-24))
    k = k * (1.0 + (a - 1.0) * k_a.astype(F32).reshape(RWKV_HEADS, HEAD_DIM))

    def step(S, inp):
        r_t, w_t, k_t, v_t, kk_t, a_t = inp
        Skk = jnp.einsum('bhvk,bhk->bhv', S, kk_t)
        S = S * w_t[:, :, None, :] - Skk[..., None] * (kk_t * a_t)[:, :, None, :] + v_t[..., None] * k_t[:, :, None, :]
        return S, jnp.einsum('bhvk,bhk->bhv', S, r_t)

    xs = tuple(jnp.swapaxes(t, 0, 1) for t in (r, decay, k, v, kk, a))
    S, o = lax.scan(step, S0.astype(F32), xs)
    o = jnp.swapaxes(o, 0, 1)
    o = head_norm(o, RWKV_GN_EPS).reshape(B, T, RWKV_WIDTH) * ln_w.astype(F32) + ln_b.astype(F32)
    bonus = jnp.sum(r * k * r_k.astype(F32)[None, None], axis=-1, keepdims=True) * v
    out = (o + bonus.reshape(B, T, RWKV_WIDTH)) * g
    return out.astype(z.dtype), S.astype(z.dtype), z[:, -1:]


def retention_mix(z, R0, pos_offset, chunk_len):
    B, T, _ = z.shape
    q, k, v, g = _split(z, RET_SPLITS)
    hd = lambda t: t.reshape(B, T, RET_HEADS, HEAD_DIM)
    pos = jnp.arange(T) + pos_offset
    q = rotary(hd(q), pos).astype(F32)
    k = rotary(hd(k), pos).astype(F32) * (HEAD_DIM ** -0.5)
    v = hd(v).astype(F32)
    L = chunk_len
    NC = T // L
    log_g = jnp.log(1.0 - jnp.power(2.0, -5.0 - jnp.arange(RET_HEADS, dtype=F32)))
    idx = jnp.arange(L, dtype=F32)
    d_intra = jnp.exp(log_g[:, None, None] * jnp.abs(idx[:, None] - idx[None, :]))
    d_state = jnp.exp(log_g[:, None] * (L - 1.0 - idx)[None, :])
    d_query = jnp.exp(log_g[:, None] * (idx + 1.0)[None, :])
    d_chunk = jnp.exp(log_g * L)
    qc = q.reshape(B, NC, L, RET_HEADS, HEAD_DIM)
    kc = k.reshape(B, NC, L, RET_HEADS, HEAD_DIM)
    vc = v.reshape(B, NC, L, RET_HEADS, HEAD_DIM)
    s = jnp.einsum('bnqhd,bnkhd->bnhqk', qc, kc) * d_intra[None, None]
    o_intra = jnp.einsum('bnhqk,bnkhe->bnqhe', s, vc)
    kv = jnp.einsum('bnkhd,bnkhe,hk->bnhde', kc, vc, d_state)

    def step(R, kv_n):
        return R * d_chunk[None, :, None, None] + kv_n, R

    R_fin, R_start = lax.scan(step, R0.astype(F32), jnp.swapaxes(kv, 0, 1))
    R_start = jnp.swapaxes(R_start, 0, 1)
    o_inter = jnp.einsum('bnqhd,bnhde->bnqhe', qc * d_query.T[None, None, :, :, None], R_start)
    o = (o_intra + o_inter).reshape(B, T, RET_HEADS, HEAD_DIM)
    o = head_norm(o, RET_GN_EPS).reshape(B, T, RET_WIDTH) * jax.nn.silu(g.astype(F32))
    return o.astype(z.dtype), R_fin.astype(z.dtype)


def layer(x, pe, hist, lw):
    (n1, f1g, f1u, f1d, nm, w_in_l, b_f, mu, w0, w_up, a0, a_up, g_up, k_k, k_a, r_k, ln_w, ln_b,
     w_out_l, n2, f2g, f2u, f2d, w_pp, pn, w_pg) = lw
    B, T, _ = x.shape
    x = x + 0.5 * swiglu(rmsnorm(x, n1), f1g, f1u, f1d)
    z = rmsnorm(x, nm) @ w_in_l
    z_fox, z_rwkv, z_ret = _split(z, (FOX_COLS, RWKV_COLS, RET_COLS))
    fq, fk, fv, ff, fg = _split(z_fox, FOX_SPLITS)
    fq, fk, fv = (t.reshape(B, T, FOX_HEADS, HEAD_DIM) for t in (fq, fk, fv))
    logf = jax.nn.log_sigmoid((ff + b_f).astype(F32))
    if hist is None:
        o_fox = fox_prompt(fq, fk, fv, logf)
        prev = jnp.zeros((B, 1, RWKV_COLS), z.dtype)
        S0 = jnp.zeros((B, RWKV_HEADS, HEAD_DIM, HEAD_DIM), F32)
        R0 = jnp.zeros((B, RET_HEADS, HEAD_DIM, HEAD_DIM), F32)
        pos0, L = 0, CHUNK
    else:
        ck, cv, clf, S0, prev, R0 = hist
        o_fox = fox_sample(fq, fk, fv, logf, ck, cv, clf)
        pos0, L = ck.shape[1], T
    o_fox = (o_fox.reshape(B, T, FOX_WIDTH) * jax.nn.sigmoid(fg)).astype(x.dtype)
    o_rwkv, S_new, shift_new = rwkv_mix(z_rwkv, prev, S0, mu, w0, w_up, a0, a_up, g_up, k_k, k_a, r_k, ln_w, ln_b)
    o_ret, R_new = retention_mix(z_ret, R0, pos0, L)
    x = x + jnp.concatenate([o_fox, o_rwkv.astype(x.dtype), o_ret.astype(x.dtype)], axis=-1) @ w_out_l
    x = x + 0.5 * swiglu(rmsnorm(x, n2), f2g, f2u, f2d)
    e = rmsnorm(pe @ w_pp, pn)
    x = x + jax.nn.sigmoid(x @ w_pg) * e
    return x, (fk.astype(x.dtype), fv.astype(x.dtype), logf.astype(x.dtype), S_new, shift_new, R_new)


def setup_inputs(seed: int = 0) -> dict:
    key = jax.random.key(seed)
    ks = iter(jax.random.split(key, 48))

    def nrm(shape, scale):
        return jax.random.normal(next(ks), shape, F32) * scale

    def gain(shape):
        return 1.0 + nrm(shape, 0.02)

    L, D = DEPTH, D_MODEL
    decay_ramp = jnp.linspace(-7.0, -2.0, RWKV_WIDTH, dtype=F32) + 0.5
    return {
        'x_prompt': nrm((BATCH, SEQ, D), 1.0),
        'x_sample': nrm((DEC_BATCH, DEC_SEQ, D), 1.0),
        'cache_fox_k': nrm((L, DEC_BATCH, PAST_LEN, FOX_HEADS, HEAD_DIM), 1.0),
        'cache_fox_v': nrm((L, DEC_BATCH, PAST_LEN, FOX_HEADS, HEAD_DIM), 1.0),
        'cache_fox_logf': jax.nn.log_sigmoid(nrm((L, DEC_BATCH, PAST_LEN, FOX_HEADS), 1.0) + FORGET_BIAS_MEAN),
        'state_rwkv': nrm((L, DEC_BATCH, RWKV_HEADS, HEAD_DIM, HEAD_DIM), 0.5),
        'state_rwkv_shift': nrm((L, DEC_BATCH, 1, RWKV_COLS), 1.0),
        'state_ret': nrm((L, DEC_BATCH, RET_HEADS, HEAD_DIM, HEAD_DIM), 0.5),
        'p_prompt': nrm((L, BATCH, SEQ, PLE_DIM), 1.0),
        'p_sample': nrm((L, DEC_BATCH, DEC_SEQ, PLE_DIM), 1.0),
        'norm_ffn1': gain((L, D)),
        'w_ffn1_gate': nrm((L, D, D_FF), D ** -0.5),
        'w_ffn1_up': nrm((L, D, D_FF), D ** -0.5),
        'w_ffn1_down': nrm((L, D_FF, D), D_FF ** -0.5),
        'norm_mix': gain((L, D)),
        'w_in': nrm((L, D, IN_COLS), D ** -0.5),
        'fox_forget_bias': FORGET_BIAS_MEAN + nrm((L, FOX_HEADS), 0.1),
        'rwkv_mu': jax.random.uniform(next(ks), (L, RWKV_COLS), F32),
        'rwkv_w0': decay_ramp[None, :] + nrm((L, RWKV_WIDTH), 0.1),
        'rwkv_w_up': nrm((L, RWKV_DECAY_RANK, RWKV_WIDTH), 0.1),
        'rwkv_a0': nrm((L, RWKV_WIDTH), 0.1),
        'rwkv_a_up': nrm((L, RWKV_A_RANK, RWKV_WIDTH), 0.1),
        'rwkv_g_up': nrm((L, RWKV_GATE_RANK, RWKV_WIDTH), RWKV_GATE_RANK ** -0.5),
        'rwkv_k_k': 0.85 + nrm((L, RWKV_WIDTH), 0.02),
        'rwkv_k_a': gain((L, RWKV_WIDTH)),
        'rwkv_r_k': nrm((L, RWKV_HEADS, HEAD_DIM), 0.1),
        'rwkv_ln_w': gain((L, RWKV_WIDTH)),
        'rwkv_ln_b': nrm((L, RWKV_WIDTH), 0.02),
        'w_out': nrm((L, MIX_WIDTH, D), MIX_WIDTH ** -0.5),
        'norm_ffn2': gain((L, D)),
        'w_ffn2_gate': nrm((L, D, D_FF), D ** -0.5),
        'w_ffn2_up': nrm((L, D, D_FF), D ** -0.5),
        'w_ffn2_down': nrm((L, D_FF, D), D_FF ** -0.5),
        'w_ple_proj': nrm((L, PLE_DIM, D), PLE_DIM ** -0.5),
        'ple_norm': gain((L, D)),
        'w_ple_gate': nrm((L, D, D), D ** -0.5),
        'norm_final': gain((D,)),
    }


def reference(x_prompt, x_sample, cache_fox_k, cache_fox_v, cache_fox_logf, state_rwkv, state_rwkv_shift, state_ret,
              p_prompt, p_sample, norm_ffn1, w_ffn1_gate, w_ffn1_up, w_ffn1_down, norm_mix, w_in, fox_forget_bias,
              rwkv_mu, rwkv_w0, rwkv_w_up, rwkv_a0, rwkv_a_up, rwkv_g_up, rwkv_k_k, rwkv_k_a, rwkv_r_k, rwkv_ln_w,
              rwkv_ln_b, w_out, norm_ffn2, w_ffn2_gate, w_ffn2_up, w_ffn2_down, w_ple_proj, ple_norm, w_ple_gate,
              norm_final):
    xp, xs = x_prompt, x_sample
    p_states, s_states = [], []
    for l in range(DEPTH):
        lw = (norm_ffn1[l], w_ffn1_gate[l], w_ffn1_up[l], w_ffn1_down[l], norm_mix[l], w_in[l], fox_forget_bias[l],
              rwkv_mu[l], rwkv_w0[l], rwkv_w_up[l], rwkv_a0[l], rwkv_a_up[l], rwkv_g_up[l], rwkv_k_k[l], rwkv_k_a[l],
              rwkv_r_k[l], rwkv_ln_w[l], rwkv_ln_b[l], w_out[l], norm_ffn2[l], w_ffn2_gate[l], w_ffn2_up[l],
              w_ffn2_down[l], w_ple_proj[l], ple_norm[l], w_ple_gate[l])
        xp, sp = layer(xp, p_prompt[l], None, lw)
        hist = (cache_fox_k[l], cache_fox_v[l], cache_fox_logf[l], state_rwkv[l], state_rwkv_shift[l], state_ret[l])
        xs, ss = layer(xs, p_sample[l], hist, lw)
        p_states.append(sp)
        s_states.append(ss)
    y_prompt = rmsnorm(xp, norm_final)
    y_sample = rmsnorm(xs, norm_final)
    prompt_fox_k = jnp.stack([s[0] for s in p_states])
    prompt_fox_v = jnp.stack([s[1] for s in p_states])
    prompt_fox_logf = jnp.stack([s[2] for s in p_states])
    prompt_rwkv = jnp.stack([s[3] for s in p_states])
    prompt_rwkv_shift = jnp.stack([s[4] for s in p_states])
    prompt_ret = jnp.stack([s[5] for s in p_states])
    sample_fox_k = jnp.stack([s[0] for s in s_states])
    sample_fox_v = jnp.stack([s[1] for s in s_states])
    sample_fox_logf = jnp.stack([s[2] for s in s_states])
    sample_rwkv = jnp.stack([s[3] for s in s_states])
    sample_rwkv_shift = jnp.stack([s[4] for s in s_states])
    sample_ret = jnp.stack([s[5] for s in s_states])
    return (y_prompt, y_sample, prompt_fox_k, prompt_fox_v, prompt_fox_logf, prompt_rwkv, prompt_rwkv_shift, prompt_ret,
            sample_fox_k, sample_fox_v, sample_fox_logf, sample_rwkv, sample_rwkv_shift, sample_ret)
```

```python
import functools

import jax
import jax.numpy as jnp
import numpy as np
from jax import lax
from jax.experimental import pallas as pl
from jax.experimental.pallas import tpu as pltpu

F32 = jnp.float32
BF16 = jnp.bfloat16

HEAD_DIM = 64
FOX_HEADS = 8
RWKV_HEADS = 4
RET_HEADS = 4
FOX_WIDTH = FOX_HEADS * HEAD_DIM
RWKV_WIDTH = RWKV_HEADS * HEAD_DIM
RET_WIDTH = RET_HEADS * HEAD_DIM
RWKV_RANK = 32
RET_CHUNK = 64
ROPE_BASE = 10000.0
NORM_EPS = 1e-6
RWKV_GN_EPS = 64e-5
RET_GN_EPS = 1e-6

LANES = 128
VMEM_LIMIT_BYTES = 58 * 1024 * 1024
DENSE_TM = 256
FFN_CHUNK = 1408
FOX_BQ = 256
CUM_BLK = 256
RWKV_TT = 256
SCAN_TC = 16
RET_BLOCK = 256

C_Q, C_K, C_V, C_G, C_F, C_RW, C_RET, C_END = 0, 512, 1024, 1536, 2048, 2176, 3072, 4096
RW_PAD = C_RET - C_RW


def _cparams(sem):
    return pltpu.CompilerParams(dimension_semantics=sem, vmem_limit_bytes=VMEM_LIMIT_BYTES)


def _resident(shape):
    nd = len(shape)
    return pl.BlockSpec(shape, lambda *_: (0,) * nd, pipeline_mode=pl.Buffered(1))


def _rms(x, g):
    return x * lax.rsqrt(jnp.mean(x * x, axis=-1, keepdims=True) + NORM_EPS) * g


def _sigmoid(x):
    return 1.0 / (1.0 + jnp.exp(-x))


def _log1pexp_neg_abs(x):
    return jnp.log(1.0 + jnp.exp(-jnp.abs(x)))


def _split2(x):
    hi = x.astype(BF16).astype(F32)
    lo = (x - hi).astype(BF16).astype(F32)
    return hi, lo


def _seg_sum(x, bd):
    hi, lo = _split2(x)
    return jnp.dot(hi, bd, preferred_element_type=F32) + jnp.dot(lo, bd, preferred_element_type=F32)


def _head_norm(o, bd, eps):
    mu = _seg_sum(o, bd) * (1.0 / HEAD_DIM)
    d = o - mu
    var = _seg_sum(d * d, bd) * (1.0 / HEAD_DIM)
    return d * lax.rsqrt(var + eps)


def _swiglu(h, wg_ref, wu_ref, wd_ref):
    d_ff = wg_ref.shape[1]
    acc = None
    for c0 in range(0, d_ff, FFN_CHUNK):
        g = jnp.dot(h, wg_ref[:, c0:c0 + FFN_CHUNK], preferred_element_type=F32)
        u = jnp.dot(h, wu_ref[:, c0:c0 + FFN_CHUNK], preferred_element_type=F32)
        a = (g * _sigmoid(g) * u).astype(BF16)
        d = jnp.dot(a, wd_ref[c0:c0 + FFN_CHUNK, :], preferred_element_type=F32)
        acc = d if acc is None else acc + d
    return acc


def _dense_in_kernel(x_ref, n1_ref, wg_ref, wu_ref, wd_ref, nm_ref, win_ref, bf_ref,
                     xo_ref, q_ref, k_ref, v_ref, g_ref, lf_ref, zr_ref, zt_ref):
    x = x_ref[...]
    x = x + 0.5 * _swiglu(_rms(x, n1_ref[...]).astype(BF16), wg_ref, wu_ref, wd_ref)
    xo_ref[...] = x
    h = _rms(x, nm_ref[...]).astype(BF16)
    z = jnp.dot(h, win_ref[...], preferred_element_type=F32)
    q_ref[...] = (z[:, C_Q:C_K] * (HEAD_DIM ** -0.5)).astype(BF16)
    k_ref[...] = z[:, C_K:C_V]
    v_ref[...] = z[:, C_V:C_G]
    g_ref[...] = z[:, C_G:C_F]
    f = z[:, C_F:C_RW] + bf_ref[...]
    lf_ref[...] = jnp.minimum(f, 0.0) - _log1pexp_neg_abs(f)
    zr_ref[...] = z[:, C_RW:C_RET]
    zt_ref[...] = z[:, C_RET:C_END]


def _dense_in(x, n1, wg, wu, wd, nm, win, bfp):
    n, d = x.shape
    tm = DENSE_TM
    row = lambda w: pl.BlockSpec((tm, w), lambda i: (i, 0))
    outs = [(d, F32), (FOX_WIDTH, BF16), (FOX_WIDTH, F32), (FOX_WIDTH, F32), (FOX_WIDTH, F32),
            (LANES, F32), (RW_PAD, F32), (C_END - C_RET, F32)]
    return pl.pallas_call(
        _dense_in_kernel,
        grid=(n // tm,),
        in_specs=[row(d), _resident(n1.shape), _resident(wg.shape), _resident(wu.shape), _resident(wd.shape),
                  _resident(nm.shape), _resident(win.shape), _resident(bfp.shape)],
        out_specs=[row(w) for w, _ in outs],
        out_shape=[jax.ShapeDtypeStruct((n, w), dt) for w, dt in outs],
        compiler_params=_cparams(("parallel",)),
        name="dense_in",
    )(x, n1, wg, wu, wd, nm, win, bfp)


def _cumsum_kernel(lf_ref, lft_ref, ccol_ref, crow_ref, *, t_len):
    nh = lft_ref.shape[1]
    r = lax.broadcasted_iota(jnp.int32, (CUM_BLK, CUM_BLK), 0)
    c = lax.broadcasted_iota(jnp.int32, (CUM_BLK, CUM_BLK), 1)
    tri_l = (r >= c).astype(F32)
    tri_u = (r <= c).astype(F32)
    carry_c = jnp.zeros((1, LANES), F32)
    carry_r = jnp.zeros((nh, 1), F32)
    for s in range(0, t_len, CUM_BLK):
        n = min(CUM_BLK, t_len - s)
        x = lf_ref[0, s:s + n, :]
        hi = x.astype(BF16).astype(F32)
        r1 = x - hi
        mid = r1.astype(BF16).astype(F32)
        lo = r1 - mid
        tl = tri_l[:n, :n]
        cc = (jnp.dot(tl, hi, preferred_element_type=F32) + jnp.dot(tl, mid, preferred_element_type=F32)
              + jnp.dot(tl, lo, preferred_element_type=F32)) + carry_c
        ccol_ref[0, s:s + n, :] = cc
        carry_c = cc[n - 1:n, :]
        y = lft_ref[0, :, s:s + n]
        hi = y.astype(BF16).astype(F32)
        r1 = y - hi
        mid = r1.astype(BF16).astype(F32)
        lo = r1 - mid
        tu = tri_u[:n, :n]
        cr = (jnp.dot(hi, tu, preferred_element_type=F32) + jnp.dot(mid, tu, preferred_element_type=F32)
              + jnp.dot(lo, tu, preferred_element_type=F32)) + carry_r
        crow_ref[0, :, s:s + n] = cr
        carry_r = cr[:, n - 1:n]


def _cumsum(lf3, lft):
    b, t, _ = lf3.shape
    nh = lft.shape[1]
    return pl.pallas_call(
        functools.partial(_cumsum_kernel, t_len=t),
        grid=(b,),
        in_specs=[pl.BlockSpec((1, t, LANES), lambda i: (i, 0, 0)), pl.BlockSpec((1, nh, t), lambda i: (i, 0, 0))],
        out_specs=[pl.BlockSpec((1, t, LANES), lambda i: (i, 0, 0)), pl.BlockSpec((1, nh, t), lambda i: (i, 0, 0))],
        out_shape=[jax.ShapeDtypeStruct((b, t, LANES), F32), jax.ShapeDtypeStruct((b, nh, t), F32)],
        compiler_params=_cparams(("parallel",)),
        name="logf_cumsum",
    )(lf3, lft)


def _softmax_pv(s_list, v_list):
    m = None
    for s in s_list:
        mi = jnp.max(s, axis=1, keepdims=True)
        m = mi if m is None else jnp.maximum(m, mi)
    l = None
    o = None
    for s, v in zip(s_list, v_list):
        p = jnp.exp(s - m)
        li = jnp.sum(p, axis=1, keepdims=True)
        oi = jnp.dot(p.astype(BF16), v, preferred_element_type=F32)
        l = li if l is None else l + li
        o = oi if o is None else o + oi
    return o / l


def _qk(q, k):
    return lax.dot_general(q, k, (((1,), (1,)), ((), ())), preferred_element_type=F32)


def _fox_prompt_kernel(q_ref, k_ref, v_ref, g_ref, ccol_ref, crow_ref, o_ref, *, t_len):
    pair = pl.program_id(1)
    lane = lax.broadcasted_iota(jnp.int32, (1, LANES), 1)
    low = lane < HEAD_DIM
    kb = k_ref[0].astype(BF16)
    vb = v_ref[0].astype(BF16)
    rr = lax.broadcasted_iota(jnp.int32, (FOX_BQ, FOX_BQ), 0)
    cc = lax.broadcasted_iota(jnp.int32, (FOX_BQ, FOX_BQ), 1)
    causal = rr >= cc
    for s0 in range(0, t_len, FOX_BQ):
        e = s0 + FOX_BQ
        q = q_ref[0, s0:e, :]
        ccol = ccol_ref[0, s0:e, :]
        res = []
        for hh in range(2):
            h = 2 * pair + hh
            msk = low if hh == 0 else jnp.logical_not(low)
            qm = jnp.where(msk, q, jnp.zeros_like(q))
            cq = jnp.sum(jnp.where(lane == h, ccol, 0.0), axis=1, keepdims=True)
            ck = crow_ref[0, pl.ds(h, 1), :]
            sd = _qk(qm, kb[s0:e]) + cq - ck[:, s0:e]
            sd = jnp.where(causal, sd, -jnp.inf)
            if s0 > 0:
                sp = _qk(qm, kb[:s0]) + cq - ck[:, :s0]
                res.append(_softmax_pv([sp, sd], [vb[:s0], vb[s0:e]]))
            else:
                res.append(_softmax_pv([sd], [vb[s0:e]]))
        o = jnp.where(low, res[0], res[1]) * _sigmoid(g_ref[0, s0:e, :])
        o_ref[0, s0:e, :] = o.astype(o_ref.dtype)


def _fox_prompt(q3, k3, v3, g3, ccol, crow):
    b, t, w = q3.shape
    npair = w // LANES
    nh = crow.shape[1]
    blk = lambda: pl.BlockSpec((1, t, LANES), lambda i, p: (i, 0, p))
    return pl.pallas_call(
        functools.partial(_fox_prompt_kernel, t_len=t),
        grid=(b, npair),
        in_specs=[blk(), blk(), blk(), blk(),
                  pl.BlockSpec((1, t, LANES), lambda i, p: (i, 0, 0)),
                  pl.BlockSpec((1, nh, t), lambda i, p: (i, 0, 0))],
        out_specs=blk(),
        out_shape=jax.ShapeDtypeStruct((b, t, w), BF16),
        compiler_params=_cparams(("parallel", "parallel")),
        name="fox_prompt",
    )(q3, k3, v3, g3, ccol, crow)


def _fox_sample_kernel(q_ref, kn_ref, vn_ref, g_ref, kc_ref, vc_ref, ccol_ref, crow_ref, o_ref, *, past, n_new):
    pair = pl.program_id(1)
    lane = lax.broadcasted_iota(jnp.int32, (1, LANES), 1)
    low = lane < HEAD_DIM
    kc = kc_ref[0, 0].astype(BF16)
    vc = vc_ref[0, 0].astype(BF16)
    kn = kn_ref[0].astype(BF16)
    vn = vn_ref[0].astype(BF16)
    q = q_ref[0]
    ccol = ccol_ref[0, past:past + n_new, :]
    rr = lax.broadcasted_iota(jnp.int32, (n_new, n_new), 0)
    cc = lax.broadcasted_iota(jnp.int32, (n_new, n_new), 1)
    causal = rr >= cc
    res = []
    for hh in range(2):
        h = 2 * pair + hh
        msk = low if hh == 0 else jnp.logical_not(low)
        qm = jnp.where(msk, q, jnp.zeros_like(q))
        cq = jnp.sum(jnp.where(lane == h, ccol, 0.0), axis=1, keepdims=True)
        ck = crow_ref[0, pl.ds(h, 1), :]
        sc = _qk(qm, kc) + cq - ck[:, :past]
        sn = _qk(qm, kn) + cq - ck[:, past:past + n_new]
        sn = jnp.where(causal, sn, -jnp.inf)
        res.append(_softmax_pv([sc, sn], [vc, vn]))
    o = jnp.where(low, res[0], res[1]) * _sigmoid(g_ref[0])
    o_ref[0] = o.astype(o_ref.dtype)


def _fox_sample(q3, k3, v3, g3, kc, vc, layer, ccol, crow):
    b, n_new, w = q3.shape
    past = kc.shape[2]
    npair = w // LANES
    nh = crow.shape[1]
    tot = past + n_new
    blk = lambda: pl.BlockSpec((1, n_new, LANES), lambda i, p: (i, 0, p))
    cache = lambda: pl.BlockSpec((1, 1, past, LANES), lambda i, p: (layer, i, 0, p))
    return pl.pallas_call(
        functools.partial(_fox_sample_kernel, past=past, n_new=n_new),
        grid=(b, npair),
        in_specs=[blk(), blk(), blk(), blk(), cache(), cache(),
                  pl.BlockSpec((1, tot, LANES), lambda i, p: (i, 0, 0)),
                  pl.BlockSpec((1, nh, tot), lambda i, p: (i, 0, 0))],
        out_specs=blk(),
        out_shape=jax.ShapeDtypeStruct((b, n_new, w), BF16),
        compiler_params=_cparams(("parallel", "parallel")),
        name="fox_sample",
    )(q3, k3, v3, g3, kc, vc, ccol, crow)


def _rwkv_pre_kernel(z_ref, zp_ref, prev_ref, mu_ref, w0_ref, a0_ref, kk_ref, ka_ref, rk_ref,
                     wup_ref, aup_ref, gup_ref, bd_ref, scan_ref, post_ref):
    i = pl.program_id(1)
    z = z_ref[0]
    tt = z.shape[0]
    p_in = jnp.where(i == 0, prev_ref[0], zp_ref[0, 7:8, :])
    row = lax.broadcasted_iota(jnp.int32, (tt, 1), 0)
    shifted = jnp.where(row == 0, p_in, pltpu.roll(z, 1, axis=0))
    zs = z + mu_ref[...] * (shifted - z)
    w_ = RWKV_WIDTH
    r, k, v, low = zs[:, 0:w_], zs[:, w_:2 * w_], zs[:, 2 * w_:3 * w_], zs[:, 3 * w_:]
    bd = bd_ref[...]
    y = w0_ref[...] + jnp.dot(jnp.tanh(low).astype(BF16), wup_ref[...], preferred_element_type=F32)
    sp = jnp.maximum(-y, 0.0) + _log1pexp_neg_abs(y)
    decay = jnp.exp(-jnp.exp(-sp - 0.5))
    a = _sigmoid(a0_ref[...] + jnp.dot(low.astype(BF16), aup_ref[...], preferred_element_type=F32))
    g = jnp.dot(_sigmoid(low).astype(BF16), gup_ref[...], preferred_element_type=F32)
    kk = k * kk_ref[...]
    kk = kk * lax.rsqrt(jnp.maximum(_seg_sum(kk * kk, bd), 1e-24))
    k2 = k * (1.0 + (a - 1.0) * ka_ref[...])
    bonus = _seg_sum(r * k2 * rk_ref[...], bd) * v
    scan_ref[0, :, 0 * w_:1 * w_] = r
    scan_ref[0, :, 1 * w_:2 * w_] = decay
    scan_ref[0, :, 2 * w_:3 * w_] = k2
    scan_ref[0, :, 3 * w_:4 * w_] = v
    scan_ref[0, :, 4 * w_:5 * w_] = kk
    scan_ref[0, :, 5 * w_:6 * w_] = kk * a
    post_ref[0, :, 0:w_] = g
    post_ref[0, :, w_:2 * w_] = bonus


def _rwkv_pre(zr3, prev, mu, w0, a0, k_k, k_a, r_k, wup, aup, gup, bd):
    b, t, w = zr3.shape
    tt = min(RWKV_TT, t)
    full = lambda a: pl.BlockSpec(a.shape, lambda i, j: (0,) * a.ndim)
    return pl.pallas_call(
        _rwkv_pre_kernel,
        grid=(b, t // tt),
        in_specs=[pl.BlockSpec((1, tt, w), lambda i, j: (i, j, 0)),
                  pl.BlockSpec((1, 8, w), lambda i, j: (i, jnp.maximum(j * (tt // 8) - 1, 0), 0)),
                  pl.BlockSpec((1, 1, w), lambda i, j: (i, 0, 0)),
                  full(mu), full(w0), full(a0), full(k_k), full(k_a), full(r_k), full(wup), full(aup), full(gup),
                  full(bd)],
        out_specs=[pl.BlockSpec((1, tt, 6 * RWKV_WIDTH), lambda i, j: (i, j, 0)),
                   pl.BlockSpec((1, tt, 2 * RWKV_WIDTH), lambda i, j: (i, j, 0))],
        out_shape=[jax.ShapeDtypeStruct((b, t, 6 * RWKV_WIDTH), F32), jax.ShapeDtypeStruct((b, t, 2 * RWKV_WIDTH), F32)],
        compiler_params=_cparams(("parallel", "parallel")),
        name="rwkv_pre",
    )(zr3, zr3, prev, mu, w0, a0, k_k, k_a, r_k, wup, aup, gup, bd)


def _rwkv_scan_kernel(x_ref, s0_ref, o_ref, s_scr):
    i = pl.program_id(0)

    @pl.when(i == 0)
    def _():
        s_scr[...] = s0_ref[...]

    def step(t, carry):
        r = x_ref[t, 0]
        w = x_ref[t, 1]
        k = x_ref[t, 2]
        kk = x_ref[t, 4]
        kka = x_ref[t, 5]
        for vi in range(HEAD_DIM):
            s = s_scr[vi]
            skk = jnp.sum(s * kk, axis=0, keepdims=True)
            vv = x_ref[t, 3, vi:vi + 1, :]
            sn = s * w - skk * kka + vv * k
            s_scr[vi] = sn
            o_ref[t, vi:vi + 1, :] = jnp.sum(sn * r, axis=0, keepdims=True)
        return carry

    lax.fori_loop(0, x_ref.shape[0], step, 0)


def _rwkv_scan(x_tm, s0_tm):
    t = x_tm.shape[0]
    tc = min(SCAN_TC, t)
    return pl.pallas_call(
        _rwkv_scan_kernel,
        grid=(t // tc,),
        in_specs=[pl.BlockSpec((tc, 6, HEAD_DIM, LANES), lambda i: (i, 0, 0, 0)),
                  pl.BlockSpec((HEAD_DIM, HEAD_DIM, LANES), lambda i: (0, 0, 0))],
        out_specs=[pl.BlockSpec((tc, HEAD_DIM, LANES), lambda i: (i, 0, 0)),
                   pl.BlockSpec((HEAD_DIM, HEAD_DIM, LANES), lambda i: (0, 0, 0))],
        out_shape=[jax.ShapeDtypeStruct((t, HEAD_DIM, LANES), F32),
                   jax.ShapeDtypeStruct((HEAD_DIM, HEAD_DIM, LANES), F32)],
        compiler_params=_cparams(("arbitrary",)),
        name="rwkv_scan",
    )(x_tm, s0_tm)


def _retention_kernel(z_ref, cos_ref, sin_ref, d_ref, dq_ref, ds_ref, gc_ref, r0_ref, bd_ref,
                      o_ref, r_scr):
    i = pl.program_id(1)

    @pl.when(i == 0)
    def _():
        r_scr[...] = r0_ref[...]

    cos = cos_ref[...]
    sin = sin_ref[...]
    lane = lax.broadcasted_iota(jnp.int32, (1, LANES), 1)
    low = lane < HEAD_DIM
    first_half = (lane % HEAD_DIM) < (HEAD_DIM // 2)
    rr = lax.broadcasted_iota(jnp.int32, (LANES, LANES), 0)
    cc = lax.broadcasted_iota(jnp.int32, (LANES, LANES), 1)
    same_head = (rr < HEAD_DIM) == (cc < HEAD_DIM)
    bd = bd_ref[...]
    w_ = RET_WIDTH

    def rot(x):
        swapped = jnp.where(first_half, pltpu.roll(x, LANES - HEAD_DIM // 2, axis=1), pltpu.roll(x, HEAD_DIM // 2, axis=1))
        return x * cos + swapped * sin

    for p in range(RET_HEADS // 2):
        c0 = p * LANES
        qr = rot(z_ref[0, :, c0:c0 + LANES])
        kr = rot(z_ref[0, :, w_ + c0:w_ + c0 + LANES]) * (HEAD_DIM ** -0.5)
        vb = z_ref[0, :, 2 * w_ + c0:2 * w_ + c0 + LANES].astype(BF16)
        gate = z_ref[0, :, 3 * w_ + c0:3 * w_ + c0 + LANES]
        qb = qr.astype(BF16)
        kb = kr.astype(BF16)
        r_old = r_scr[0, p]
        o = jnp.dot((qr * dq_ref[p]).astype(BF16), r_old.astype(BF16), preferred_element_type=F32)
        for hh in range(2):
            msk = low if hh == 0 else jnp.logical_not(low)
            qm = jnp.where(msk, qb, jnp.zeros_like(qb))
            pmat = (_qk(qm, kb) * d_ref[2 * p + hh]).astype(BF16)
            o = o + jnp.where(msk, jnp.dot(pmat, vb, preferred_element_type=F32), 0.0)
        kd = (kr * ds_ref[p]).astype(BF16)
        kv = lax.dot_general(kd, vb, (((0,), (0,)), ((), ())), preferred_element_type=F32)
        r_scr[0, p] = r_old * gc_ref[p] + jnp.where(same_head, kv, 0.0)
        out = _head_norm(o, bd, RET_GN_EPS) * (gate * _sigmoid(gate))
        o_ref[0, :, c0:c0 + LANES] = out.astype(o_ref.dtype)


def _retention(zt3, cos, sin, dmat, dq, ds, gc, r0, bd, blk):
    b, t, w = zt3.shape
    npair = RET_HEADS // 2
    full = lambda a: pl.BlockSpec(a.shape, lambda i, j: (0,) * a.ndim)
    return pl.pallas_call(
        _retention_kernel,
        grid=(b, t // blk),
        in_specs=[pl.BlockSpec((1, blk, w), lambda i, j: (i, j, 0)),
                  pl.BlockSpec((blk, LANES), lambda i, j: (j, 0)),
                  pl.BlockSpec((blk, LANES), lambda i, j: (j, 0)),
                  full(dmat), full(dq), full(ds), full(gc),
                  pl.BlockSpec((1, npair, LANES, LANES), lambda i, j: (i, 0, 0, 0)),
                  full(bd)],
        out_specs=[pl.BlockSpec((1, blk, RET_WIDTH), lambda i, j: (i, j, 0)),
                   pl.BlockSpec((1, npair, LANES, LANES), lambda i, j: (i, 0, 0, 0))],
        out_shape=[jax.ShapeDtypeStruct((b, t, RET_WIDTH), BF16),
                   jax.ShapeDtypeStruct((b, npair, LANES, LANES), F32)],
        compiler_params=_cparams(("parallel", "arbitrary")),
        name="retention",
    )(zt3, cos, sin, dmat, dq, ds, gc, r0, bd)


def _retention_tables(blk, chunk, t_len, pos0):
    log_g = jnp.log(1.0 - jnp.power(2.0, -5.0 - jnp.arange(RET_HEADS, dtype=F32)))
    idx = jnp.arange(blk)
    ci = idx // chunk
    diff = (idx[:, None] - idx[None, :]).astype(F32)
    expo = jnp.where(ci[:, None] == ci[None, :], jnp.abs(diff), diff)
    dmat = jnp.exp(log_g[:, None, None] * expo[None])
    dmat = jnp.where((ci[:, None] >= ci[None, :])[None], dmat, 0.0)
    lg_lane = jnp.repeat(log_g, HEAD_DIM).reshape(RET_HEADS // 2, 1, LANES)
    tau = idx.astype(F32)[None, :, None]
    dq = jnp.exp(lg_lane * (tau + 1.0))
    ds = jnp.exp(lg_lane * (blk - 1.0 - tau))
    gc = jnp.exp(lg_lane * float(blk))
    half = HEAD_DIM // 2
    inv = 1.0 / (ROPE_BASE ** (jnp.arange(half, dtype=F32) / half))
    ang = (jnp.arange(t_len) + pos0).astype(F32)[:, None] * inv[None, :]
    cos = jnp.tile(jnp.cos(ang), (1, LANES // half))
    sin = jnp.sin(ang)
    sin = jnp.tile(jnp.concatenate([-sin, sin], axis=1), (1, LANES // HEAD_DIM))
    return cos, sin, dmat, dq, ds, gc


def _dense_out_kernel(x_ref, of_ref, orw_ref, gb_ref, oret_ref, pe_ref, lnw_ref, lnb_ref, bd_ref,
                      wo1_ref, wo2_ref, wo3_ref, n2_ref, wg_ref, wu_ref, wd_ref, wpp_ref, pn_ref, wpg_ref, nf_ref,
                      xo_ref, *, final):
    w_ = RWKV_WIDTH
    hn = _head_norm(orw_ref[...], bd_ref[...], RWKV_GN_EPS)
    orw = ((hn * lnw_ref[...] + lnb_ref[...] + gb_ref[:, w_:2 * w_]) * gb_ref[:, 0:w_]).astype(BF16)
    x = x_ref[...]
    x = x + (jnp.dot(of_ref[...], wo1_ref[...], preferred_element_type=F32)
             + jnp.dot(orw, wo2_ref[...], preferred_element_type=F32)
             + jnp.dot(oret_ref[...], wo3_ref[...], preferred_element_type=F32))
    x = x + 0.5 * _swiglu(_rms(x, n2_ref[...]).astype(BF16), wg_ref, wu_ref, wd_ref)
    e = _rms(jnp.dot(pe_ref[...].astype(BF16), wpp_ref[...], preferred_element_type=F32), pn_ref[...])
    x = x + _sigmoid(jnp.dot(x.astype(BF16), wpg_ref[...], preferred_element_type=F32)) * e
    if final:
        x = _rms(x, nf_ref[...])
    xo_ref[...] = x


def _dense_out(x, of, orw, gb, oret, pe, lnw, lnb, bd, wo1, wo2, wo3, n2, wg, wu, wd, wpp, pn, wpg, nf, final):
    n, d = x.shape
    tm = DENSE_TM
    row = lambda a: pl.BlockSpec((tm, a.shape[1]), lambda i: (i, 0))
    consts = (lnw, lnb, bd, wo1, wo2, wo3, n2, wg, wu, wd, wpp, pn, wpg, nf)
    return pl.pallas_call(
        functools.partial(_dense_out_kernel, final=final),
        grid=(n // tm,),
        in_specs=[row(x), row(of), row(orw), row(gb), row(oret), row(pe)] + [_resident(c.shape) for c in consts],
        out_specs=row(x),
        out_shape=jax.ShapeDtypeStruct((n, d), F32),
        compiler_params=_cparams(("parallel",)),
        name="dense_out",
    )(x, of, orw, gb, oret, pe, *consts)


def _to_time_major(a, nfield):
    b, t, _ = a.shape
    a = a.reshape(b, t, nfield, RWKV_HEADS, HEAD_DIM)
    return a.transpose(1, 2, 4, 3, 0).reshape(t, nfield, HEAD_DIM, RWKV_HEADS * b)


def _block_diag_ones(width):
    idx = np.arange(width) // HEAD_DIM
    return jnp.asarray(idx[:, None] == idx[None, :], F32)


def _group(x, pe_l, hist, lw, tables, layer, final):
    (n1, wg1, wu1, wd1, nm, win, bfp, mu, w0, a0, k_k, k_a, r_k, wup, aup, gup, lnw, lnb,
     wo1, wo2, wo3, n2, wg2, wu2, wd2, wpp, pn, wpg, nf) = lw
    b, t, d = x.shape
    n = b * t
    bd256, bd128 = _block_diag_ones(RWKV_WIDTH), _block_diag_ones(LANES)
    x2, q, k, v, fg, lf, zr, zt = _dense_in(x.reshape(n, d), n1, wg1, wu1, wd1, nm, win, bfp)
    r3 = lambda a: a.reshape(b, t, a.shape[-1])
    lf3 = r3(lf)
    logf = lf3[:, :, :FOX_HEADS]
    zr3 = r3(zr)
    if hist is None:
        lft = logf.transpose(0, 2, 1)
        ccol, crow = _cumsum(lf3, lft)
        of = _fox_prompt(r3(q), r3(k), r3(v), r3(fg), ccol, crow)
        prev = jnp.zeros((b, 1, RW_PAD), F32)
        s0 = jnp.zeros((HEAD_DIM, HEAD_DIM, RWKV_HEADS * b), F32)
        r0 = jnp.zeros((b, RET_HEADS // 2, LANES, LANES), F32)
        blk = RET_BLOCK
    else:
        kc, vc, clf, s0, prev, r0 = hist
        past = clf.shape[1]
        lf_all = jnp.concatenate([jnp.pad(clf, ((0, 0), (0, 0), (0, LANES - FOX_HEADS))), lf3], axis=1)
        lft = lf_all[:, :, :FOX_HEADS].transpose(0, 2, 1)
        ccol, crow = _cumsum(lf_all, lft)
        of = _fox_sample(r3(q), r3(k), r3(v), r3(fg), kc, vc, layer, ccol, crow)
        blk = t
    cos, sin, dmat, dq, ds, gc = tables
    scan_in, post_in = _rwkv_pre(zr3, prev, mu, w0, a0, k_k, k_a, r_k, wup, aup, gup, bd256)
    o_tm, s_fin = _rwkv_scan(_to_time_major(scan_in, 6), s0)
    orw = o_tm.reshape(t, HEAD_DIM, RWKV_HEADS, b).transpose(3, 0, 2, 1).reshape(n, RWKV_WIDTH)
    oret, r_fin = _retention(r3(zt), cos, sin, dmat, dq, ds, gc, r0, bd128, blk)
    xo = _dense_out(x2, of.reshape(n, FOX_WIDTH), orw, post_in.reshape(n, 2 * RWKV_WIDTH), oret.reshape(n, RET_WIDTH),
                    pe_l.reshape(n, pe_l.shape[-1]), lnw, lnb, bd256, wo1, wo2, wo3, n2, wg2, wu2, wd2, wpp, pn, wpg,
                    nf, final)
    s_new = s_fin.reshape(HEAD_DIM, HEAD_DIM, RWKV_HEADS, b).transpose(3, 2, 0, 1)
    r_new = jnp.stack([r_fin[:, hh // 2, (hh % 2) * HEAD_DIM:(hh % 2 + 1) * HEAD_DIM,
                             (hh % 2) * HEAD_DIM:(hh % 2 + 1) * HEAD_DIM] for hh in range(RET_HEADS)], axis=1)
    state = (r3(k).reshape(b, t, FOX_HEADS, HEAD_DIM), r3(v).reshape(b, t, FOX_HEADS, HEAD_DIM), logf,
             s_new, zr3[:, t - 1:t, :], r_new)
    return xo.reshape(b, t, d), state


def kernel(x_prompt, x_sample, cache_fox_k, cache_fox_v, cache_fox_logf, state_rwkv, state_rwkv_shift, state_ret,
           p_prompt, p_sample, norm_ffn1, w_ffn1_gate, w_ffn1_up, w_ffn1_down, norm_mix, w_in, fox_forget_bias,
           rwkv_mu, rwkv_w0, rwkv_w_up, rwkv_a0, rwkv_a_up, rwkv_g_up, rwkv_k_k, rwkv_k_a, rwkv_r_k, rwkv_ln_w,
           rwkv_ln_b, w_out, norm_ffn2, w_ffn2_gate, w_ffn2_up, w_ffn2_down, w_ple_proj, ple_norm, w_ple_gate,
           norm_final):
    depth, d, _ = w_in.shape
    bp, tp, _ = x_prompt.shape
    bs, ts, _ = x_sample.shape
    past = cache_fox_k.shape[2]
    assert RWKV_HEADS * bp == LANES and RWKV_HEADS * bs == LANES, "rwkv scan puts (head, batch) on the 128 lanes"
    assert tp % RET_BLOCK == 0 and tp % FOX_BQ == 0 and (bp * tp) % DENSE_TM == 0 and (bs * ts) % DENSE_TM == 0

    fw, rw = FOX_WIDTH, RWKV_WIDTH
    o_f = 3 * fw
    o_g = o_f + FOX_HEADS
    o_rw = o_g + fw
    o_ret = o_rw + 3 * rw + 3 * RWKV_RANK
    zpad = lambda w: jnp.zeros((depth, d, w), F32)
    r_ = RWKV_RANK
    rw_cols = 3 * rw + 3 * r_

    def regroup(a):
        parts = [a[..., 0:rw], a[..., rw + r_:2 * rw + r_], a[..., 2 * rw + r_:3 * rw + r_], a[..., rw:rw + r_],
                 a[..., 3 * rw + r_:rw_cols], jnp.zeros(a.shape[:-1] + (RW_PAD - rw_cols,), a.dtype)]
        return jnp.concatenate(parts, axis=-1)

    def ungroup(a):
        parts = [a[..., 0:rw], a[..., 3 * rw:3 * rw + r_], a[..., rw:3 * rw], a[..., 3 * rw + r_:rw_cols]]
        return jnp.concatenate(parts, axis=-1)

    win_all = jnp.concatenate([w_in[:, :, 0:o_f], w_in[:, :, o_g:o_rw], w_in[:, :, o_f:o_g], zpad(LANES - FOX_HEADS),
                               regroup(w_in[:, :, o_rw:o_ret]), w_in[:, :, o_ret:]], axis=2).astype(BF16)
    bfp = jnp.pad(fox_forget_bias, ((0, 0), (0, LANES - FOX_HEADS)))[:, None, :]
    mu_p = regroup(rwkv_mu)[:, None, :]
    lowrank = lambda w, slot: jnp.pad(w, ((0, 0), (slot * RWKV_RANK, LANES - (slot + 1) * RWKV_RANK), (0, 0))).astype(BF16)
    wup, aup, gup = lowrank(rwkv_w_up, 0), lowrank(rwkv_a_up, 1), lowrank(rwkv_g_up, 2)
    vec = lambda a: a.reshape(depth, 1, -1)
    bf = lambda a: a.astype(BF16)
    wg1, wu1, wd1 = bf(w_ffn1_gate), bf(w_ffn1_up), bf(w_ffn1_down)
    wg2, wu2, wd2 = bf(w_ffn2_gate), bf(w_ffn2_up), bf(w_ffn2_down)
    wo = bf(w_out)
    wpp, wpg = bf(w_ple_proj), bf(w_ple_gate)
    nf = norm_final.reshape(1, -1)

    def layer_weights(l):
        return (vec(norm_ffn1)[l], wg1[l], wu1[l], wd1[l], vec(norm_mix)[l], win_all[l], bfp[l], mu_p[l],
                vec(rwkv_w0)[l], vec(rwkv_a0)[l], vec(rwkv_k_k)[l], vec(rwkv_k_a)[l], vec(rwkv_r_k)[l],
                wup[l], aup[l], gup[l], vec(rwkv_ln_w)[l], vec(rwkv_ln_b)[l],
                wo[l, 0:fw], wo[l, fw:fw + rw], wo[l, fw + rw:], vec(norm_ffn2)[l], wg2[l], wu2[l], wd2[l],
                wpp[l], vec(ple_norm)[l], wpg[l], nf)

    kc = cache_fox_k.reshape(depth, bs, past, fw)
    vc = cache_fox_v.reshape(depth, bs, past, fw)
    s0_all = state_rwkv.transpose(0, 3, 4, 2, 1).reshape(depth, HEAD_DIM, HEAD_DIM, RWKV_HEADS * bs)
    prev_all = regroup(state_rwkv_shift)
    eye2 = jnp.eye(2, dtype=F32)
    r0_all = jnp.einsum('lbpqde,qr->lbpqdre', state_ret.reshape(depth, bs, RET_HEADS // 2, 2, HEAD_DIM, HEAD_DIM),
                        eye2).reshape(depth, bs, RET_HEADS // 2, LANES, LANES)

    tab_p = _retention_tables(RET_BLOCK, RET_CHUNK, tp, 0)
    tab_s = _retention_tables(ts, ts, ts, past)

    xp, xs = x_prompt, x_sample
    p_states, s_states = [], []
    for l in range(depth):
        lw = layer_weights(l)
        final = l == depth - 1
        xp, sp = _group(xp, p_prompt[l], None, lw, tab_p, l, final)
        hist = (kc, vc, cache_fox_logf[l], s0_all[l], prev_all[l], r0_all[l])
        xs, ss = _group(xs, p_sample[l], hist, lw, tab_s, l, final)
        p_states.append(sp)
        s_states.append(ss)

    def stacked(states):
        cols = [jnp.stack([s[i] for s in states]) for i in range(6)]
        cols[4] = ungroup(cols[4])
        return cols

    return (xp, xs, *stacked(p_states), *stacked(s_states))
```

```python
import functools

import jax
import jax.numpy as jnp
import numpy as np
from jax import lax
from jax.experimental import pallas as pl
from jax.experimental.pallas import tpu as pltpu

F32 = jnp.float32
BF16 = jnp.bfloat16

HEAD_DIM = 64
FOX_HEADS = 8
RWKV_HEADS = 4
RET_HEADS = 4
FOX_WIDTH = FOX_HEADS * HEAD_DIM
RWKV_WIDTH = RWKV_HEADS * HEAD_DIM
RET_WIDTH = RET_HEADS * HEAD_DIM
RWKV_RANK = 32
RET_CHUNK = 64
ROPE_BASE = 10000.0
NORM_EPS = 1e-6
RWKV_GN_EPS = 64e-5
RET_GN_EPS = 1e-6

LANES = 128
VMEM_LIMIT_BYTES = 58 * 1024 * 1024
DENSE_TM = 256
FFN_CHUNK = 1408
FOX_BQ = 256
CUM_BLK = 256
RWKV_TT = 256
SCAN_TC = 8
SCAN_K_UNROLL = 8
SCAN_PAIRS = 3
F_R, F_W, F_K, F_V, F_KK, F_KKA = range(6)
RET_BLOCK = 256

C_Q, C_K, C_V, C_G, C_F, C_RW, C_RET, C_END = 0, 512, 1024, 1536, 2048, 2176, 3072, 4096
RW_PAD = C_RET - C_RW


def _cparams(sem):
    return pltpu.CompilerParams(dimension_semantics=sem, vmem_limit_bytes=VMEM_LIMIT_BYTES)


def _resident(shape):
    nd = len(shape)
    return pl.BlockSpec(shape, lambda *_: (0,) * nd, pipeline_mode=pl.Buffered(1))


def _rms(x, g):
    return x * lax.rsqrt(jnp.mean(x * x, axis=-1, keepdims=True) + NORM_EPS) * g


def _sigmoid(x):
    return 1.0 / (1.0 + jnp.exp(-x))


def _log1pexp_neg_abs(x):
    return jnp.log(1.0 + jnp.exp(-jnp.abs(x)))


def _split2(x):
    hi = x.astype(BF16).astype(F32)
    lo = (x - hi).astype(BF16).astype(F32)
    return hi, lo


def _seg_sum(x, bd):
    hi, lo = _split2(x)
    return jnp.dot(hi, bd, preferred_element_type=F32) + jnp.dot(lo, bd, preferred_element_type=F32)


def _head_norm(o, bd, eps):
    mu = _seg_sum(o, bd) * (1.0 / HEAD_DIM)
    d = o - mu
    var = _seg_sum(d * d, bd) * (1.0 / HEAD_DIM)
    return d * lax.rsqrt(var + eps)


def _swiglu(h, wg_ref, wu_ref, wd_ref):
    d_ff = wg_ref.shape[1]
    acc = None
    for c0 in range(0, d_ff, FFN_CHUNK):
        g = jnp.dot(h, wg_ref[:, c0:c0 + FFN_CHUNK], preferred_element_type=F32)
        u = jnp.dot(h, wu_ref[:, c0:c0 + FFN_CHUNK], preferred_element_type=F32)
        a = (g * _sigmoid(g) * u).astype(BF16)
        d = jnp.dot(a, wd_ref[c0:c0 + FFN_CHUNK, :], preferred_element_type=F32)
        acc = d if acc is None else acc + d
    return acc


def _dense_in_kernel(x_ref, n1_ref, wg_ref, wu_ref, wd_ref, nm_ref, win_ref, bf_ref,
                     xo_ref, q_ref, k_ref, v_ref, g_ref, lf_ref, zr_ref, zt_ref):
    x = x_ref[...]
    x = x + 0.5 * _swiglu(_rms(x, n1_ref[...]).astype(BF16), wg_ref, wu_ref, wd_ref)
    xo_ref[...] = x
    h = _rms(x, nm_ref[...]).astype(BF16)
    z = jnp.dot(h, win_ref[...], preferred_element_type=F32)
    q_ref[...] = (z[:, C_Q:C_K] * (HEAD_DIM ** -0.5)).astype(BF16)
    k_ref[...] = z[:, C_K:C_V]
    v_ref[...] = z[:, C_V:C_G]
    g_ref[...] = z[:, C_G:C_F]
    f = z[:, C_F:C_RW] + bf_ref[...]
    lf_ref[...] = jnp.minimum(f, 0.0) - _log1pexp_neg_abs(f)
    zr_ref[...] = z[:, C_RW:C_RET]
    zt_ref[...] = z[:, C_RET:C_END]


def _dense_in(x, n1, wg, wu, wd, nm, win, bfp):
    n, d = x.shape
    tm = DENSE_TM
    row = lambda w: pl.BlockSpec((tm, w), lambda i: (i, 0))
    outs = [(d, F32), (FOX_WIDTH, BF16), (FOX_WIDTH, F32), (FOX_WIDTH, F32), (FOX_WIDTH, F32),
            (LANES, F32), (RW_PAD, F32), (C_END - C_RET, F32)]
    return pl.pallas_call(
        _dense_in_kernel,
        grid=(n // tm,),
        in_specs=[row(d), _resident(n1.shape), _resident(wg.shape), _resident(wu.shape), _resident(wd.shape),
                  _resident(nm.shape), _resident(win.shape), _resident(bfp.shape)],
        out_specs=[row(w) for w, _ in outs],
        out_shape=[jax.ShapeDtypeStruct((n, w), dt) for w, dt in outs],
        compiler_params=_cparams(("parallel",)),
        name="dense_in",
    )(x, n1, wg, wu, wd, nm, win, bfp)


def _cumsum_kernel(lf_ref, lft_ref, ccol_ref, crow_ref, *, t_len):
    nh = lft_ref.shape[1]
    r = lax.broadcasted_iota(jnp.int32, (CUM_BLK, CUM_BLK), 0)
    c = lax.broadcasted_iota(jnp.int32, (CUM_BLK, CUM_BLK), 1)
    tri_l = (r >= c).astype(F32)
    tri_u = (r <= c).astype(F32)
    carry_c = jnp.zeros((1, LANES), F32)
    carry_r = jnp.zeros((nh, 1), F32)
    for s in range(0, t_len, CUM_BLK):
        n = min(CUM_BLK, t_len - s)
        x = lf_ref[0, s:s + n, :]
        hi = x.astype(BF16).astype(F32)
        r1 = x - hi
        mid = r1.astype(BF16).astype(F32)
        lo = r1 - mid
        tl = tri_l[:n, :n]
        cc = (jnp.dot(tl, hi, preferred_element_type=F32) + jnp.dot(tl, mid, preferred_element_type=F32)
              + jnp.dot(tl, lo, preferred_element_type=F32)) + carry_c
        ccol_ref[0, s:s + n, :] = cc
        carry_c = cc[n - 1:n, :]
        y = lft_ref[0, :, s:s + n]
        hi = y.astype(BF16).astype(F32)
        r1 = y - hi
        mid = r1.astype(BF16).astype(F32)
        lo = r1 - mid
        tu = tri_u[:n, :n]
        cr = (jnp.dot(hi, tu, preferred_element_type=F32) + jnp.dot(mid, tu, preferred_element_type=F32)
              + jnp.dot(lo, tu, preferred_element_type=F32)) + carry_r
        crow_ref[0, :, s:s + n] = cr
        carry_r = cr[:, n - 1:n]


def _cumsum(lf3, lft):
    b, t, _ = lf3.shape
    nh = lft.shape[1]
    return pl.pallas_call(
        functools.partial(_cumsum_kernel, t_len=t),
        grid=(b,),
        in_specs=[pl.BlockSpec((1, t, LANES), lambda i: (i, 0, 0)), pl.BlockSpec((1, nh, t), lambda i: (i, 0, 0))],
        out_specs=[pl.BlockSpec((1, t, LANES), lambda i: (i, 0, 0)), pl.BlockSpec((1, nh, t), lambda i: (i, 0, 0))],
        out_shape=[jax.ShapeDtypeStruct((b, t, LANES), F32), jax.ShapeDtypeStruct((b, nh, t), F32)],
        compiler_params=_cparams(("parallel",)),
        name="logf_cumsum",
    )(lf3, lft)


def _softmax_pv(s_list, v_list):
    m = None
    for s in s_list:
        mi = jnp.max(s, axis=1, keepdims=True)
        m = mi if m is None else jnp.maximum(m, mi)
    l = None
    o = None
    for s, v in zip(s_list, v_list):
        p = jnp.exp(s - m)
        li = jnp.sum(p, axis=1, keepdims=True)
        oi = jnp.dot(p.astype(BF16), v, preferred_element_type=F32)
        l = li if l is None else l + li
        o = oi if o is None else o + oi
    return o / l


def _qk(q, k):
    return lax.dot_general(q, k, (((1,), (1,)), ((), ())), preferred_element_type=F32)


def _fox_prompt_kernel(q_ref, k_ref, v_ref, g_ref, ccol_ref, crow_ref, o_ref, *, t_len):
    pair = pl.program_id(1)
    lane = lax.broadcasted_iota(jnp.int32, (1, LANES), 1)
    low = lane < HEAD_DIM
    kb = k_ref[0].astype(BF16)
    vb = v_ref[0].astype(BF16)
    rr = lax.broadcasted_iota(jnp.int32, (FOX_BQ, FOX_BQ), 0)
    cc = lax.broadcasted_iota(jnp.int32, (FOX_BQ, FOX_BQ), 1)
    causal = rr >= cc
    for s0 in range(0, t_len, FOX_BQ):
        e = s0 + FOX_BQ
        q = q_ref[0, s0:e, :]
        ccol = ccol_ref[0, s0:e, :]
        res = []
        for hh in range(2):
            h = 2 * pair + hh
            msk = low if hh == 0 else jnp.logical_not(low)
            qm = jnp.where(msk, q, jnp.zeros_like(q))
            cq = jnp.sum(jnp.where(lane == h, ccol, 0.0), axis=1, keepdims=True)
            ck = crow_ref[0, pl.ds(h, 1), :]
            sd = _qk(qm, kb[s0:e]) + cq - ck[:, s0:e]
            sd = jnp.where(causal, sd, -jnp.inf)
            if s0 > 0:
                sp = _qk(qm, kb[:s0]) + cq - ck[:, :s0]
                res.append(_softmax_pv([sp, sd], [vb[:s0], vb[s0:e]]))
            else:
                res.append(_softmax_pv([sd], [vb[s0:e]]))
        o = jnp.where(low, res[0], res[1]) * _sigmoid(g_ref[0, s0:e, :])
        o_ref[0, s0:e, :] = o.astype(o_ref.dtype)


def _fox_prompt(q3, k3, v3, g3, ccol, crow):
    b, t, w = q3.shape
    npair = w // LANES
    nh = crow.shape[1]
    blk = lambda: pl.BlockSpec((1, t, LANES), lambda i, p: (i, 0, p))
    return pl.pallas_call(
        functools.partial(_fox_prompt_kernel, t_len=t),
        grid=(b, npair),
        in_specs=[blk(), blk(), blk(), blk(),
                  pl.BlockSpec((1, t, LANES), lambda i, p: (i, 0, 0)),
                  pl.BlockSpec((1, nh, t), lambda i, p: (i, 0, 0))],
        out_specs=blk(),
        out_shape=jax.ShapeDtypeStruct((b, t, w), BF16),
        compiler_params=_cparams(("parallel", "parallel")),
        name="fox_prompt",
    )(q3, k3, v3, g3, ccol, crow)


def _fox_sample_kernel(q_ref, kn_ref, vn_ref, g_ref, kc_ref, vc_ref, ccol_ref, crow_ref, o_ref, *, past, n_new):
    pair = pl.program_id(1)
    lane = lax.broadcasted_iota(jnp.int32, (1, LANES), 1)
    low = lane < HEAD_DIM
    kc = kc_ref[0, 0].astype(BF16)
    vc = vc_ref[0, 0].astype(BF16)
    kn = kn_ref[0].astype(BF16)
    vn = vn_ref[0].astype(BF16)
    q = q_ref[0]
    ccol = ccol_ref[0, past:past + n_new, :]
    rr = lax.broadcasted_iota(jnp.int32, (n_new, n_new), 0)
    cc = lax.broadcasted_iota(jnp.int32, (n_new, n_new), 1)
    causal = rr >= cc
    res = []
    for hh in range(2):
        h = 2 * pair + hh
        msk = low if hh == 0 else jnp.logical_not(low)
        qm = jnp.where(msk, q, jnp.zeros_like(q))
        cq = jnp.sum(jnp.where(lane == h, ccol, 0.0), axis=1, keepdims=True)
        ck = crow_ref[0, pl.ds(h, 1), :]
        sc = _qk(qm, kc) + cq - ck[:, :past]
        sn = _qk(qm, kn) + cq - ck[:, past:past + n_new]
        sn = jnp.where(causal, sn, -jnp.inf)
        res.append(_softmax_pv([sc, sn], [vc, vn]))
    o = jnp.where(low, res[0], res[1]) * _sigmoid(g_ref[0])
    o_ref[0] = o.astype(o_ref.dtype)


def _fox_sample(q3, k3, v3, g3, kc, vc, layer, ccol, crow):
    b, n_new, w = q3.shape
    past = kc.shape[2]
    npair = w // LANES
    nh = crow.shape[1]
    tot = past + n_new
    blk = lambda: pl.BlockSpec((1, n_new, LANES), lambda i, p: (i, 0, p))
    cache = lambda: pl.BlockSpec((1, 1, past, LANES), lambda i, p: (layer, i, 0, p))
    return pl.pallas_call(
        functools.partial(_fox_sample_kernel, past=past, n_new=n_new),
        grid=(b, npair),
        in_specs=[blk(), blk(), blk(), blk(), cache(), cache(),
                  pl.BlockSpec((1, tot, LANES), lambda i, p: (i, 0, 0)),
                  pl.BlockSpec((1, nh, tot), lambda i, p: (i, 0, 0))],
        out_specs=blk(),
        out_shape=jax.ShapeDtypeStruct((b, n_new, w), BF16),
        compiler_params=_cparams(("parallel", "parallel")),
        name="fox_sample",
    )(q3, k3, v3, g3, kc, vc, ccol, crow)


def _rwkv_pre_kernel(z_ref, zp_ref, prev_ref, mu_ref, w0_ref, a0_ref, kk_ref, ka_ref, rk_ref,
                     wup_ref, aup_ref, gup_ref, bd_ref, scan_ref, post_ref):
    i = pl.program_id(1)
    z = z_ref[0]
    tt = z.shape[0]
    p_in = jnp.where(i == 0, prev_ref[0], zp_ref[0, 7:8, :])
    row = lax.broadcasted_iota(jnp.int32, (tt, 1), 0)
    shifted = jnp.where(row == 0, p_in, pltpu.roll(z, 1, axis=0))
    zs = z + mu_ref[...] * (shifted - z)
    w_ = RWKV_WIDTH
    r, k, v, low = zs[:, 0:w_], zs[:, w_:2 * w_], zs[:, 2 * w_:3 * w_], zs[:, 3 * w_:]
    bd = bd_ref[...]
    y = w0_ref[...] + jnp.dot(jnp.tanh(low).astype(BF16), wup_ref[...], preferred_element_type=F32)
    sp = jnp.maximum(-y, 0.0) + _log1pexp_neg_abs(y)
    decay = jnp.exp(-jnp.exp(-sp - 0.5))
    a = _sigmoid(a0_ref[...] + jnp.dot(low.astype(BF16), aup_ref[...], preferred_element_type=F32))
    g = jnp.dot(_sigmoid(low).astype(BF16), gup_ref[...], preferred_element_type=F32)
    kk = k * kk_ref[...]
    kk = kk * lax.rsqrt(jnp.maximum(_seg_sum(kk * kk, bd), 1e-24))
    k2 = k * (1.0 + (a - 1.0) * ka_ref[...])
    bonus = _seg_sum(r * k2 * rk_ref[...], bd) * v
    fields = {F_R: r, F_W: decay, F_K: k2, F_V: v, F_KK: kk, F_KKA: kk * a}
    per_head = 2 * SCAN_PAIRS * HEAD_DIM
    for h in range(RWKV_HEADS):
        for f, val in fields.items():
            c0 = h * per_head + f * HEAD_DIM
            scan_ref[:, c0:c0 + HEAD_DIM] = val[:, h * HEAD_DIM:(h + 1) * HEAD_DIM]
    post_ref[0, :, 0:w_] = g
    post_ref[0, :, w_:2 * w_] = bonus


def _rwkv_pre(zr3, prev, mu, w0, a0, k_k, k_a, r_k, wup, aup, gup, bd):
    b, t, w = zr3.shape
    tt = min(RWKV_TT, t)
    full = lambda a: pl.BlockSpec(a.shape, lambda i, j: (0,) * a.ndim)
    return pl.pallas_call(
        _rwkv_pre_kernel,
        grid=(b, t // tt),
        in_specs=[pl.BlockSpec((1, tt, w), lambda i, j: (i, j, 0)),
                  pl.BlockSpec((1, 8, w), lambda i, j: (i, jnp.maximum(j * (tt // 8) - 1, 0), 0)),
                  pl.BlockSpec((1, 1, w), lambda i, j: (i, 0, 0)),
                  full(mu), full(w0), full(a0), full(k_k), full(k_a), full(r_k), full(wup), full(aup), full(gup),
                  full(bd)],
        out_specs=[pl.BlockSpec((tt, 6 * RWKV_WIDTH), lambda i, j: (j, i)),
                   pl.BlockSpec((1, tt, 2 * RWKV_WIDTH), lambda i, j: (i, j, 0))],
        out_shape=[jax.ShapeDtypeStruct((t, b * 6 * RWKV_WIDTH), F32), jax.ShapeDtypeStruct((b, t, 2 * RWKV_WIDTH), F32)],
        compiler_params=_cparams(("parallel", "parallel")),
        name="rwkv_pre",
    )(zr3, zr3, prev, mu, w0, a0, k_k, k_a, r_k, wup, aup, gup, bd)


def _rwkv_scan_kernel(*refs):
    x_ref, s0_ref, o_ref, s_ref, x_scr, o_scr = refs
    i = pl.program_id(0)
    tc, nb = x_ref.shape[0], x_ref.shape[1]
    ng = HEAD_DIM // 8

    @pl.when(i == 0)
    def _():
        s_ref[...] = s0_ref[...]

    for t in range(tc):
        for fp in range(SCAN_PAIRS):
            cols = [(h * SCAN_PAIRS + fp) * LANES for h in range(RWKV_HEADS)]
            y = jnp.concatenate([x_ref[t, :, c:c + LANES] for c in cols], axis=0)
            yt = y.T
            x_scr[t, 2 * fp] = yt[0:HEAD_DIM]
            x_scr[t, 2 * fp + 1] = yt[HEAD_DIM:2 * HEAD_DIM]

    def step(t, carry):
        def dot_kk(k, acc):
            kk = x_scr[t, F_KK, pl.ds(k, 1), :]
            return tuple(acc[g] + s_ref[k, g * 8:(g + 1) * 8, :] * kk for g in range(ng))

        zero = tuple(jnp.zeros((8, LANES), F32) for _ in range(ng))
        skk = lax.fori_loop(0, HEAD_DIM, dot_kk, zero, unroll=SCAN_K_UNROLL)
        vv = tuple(x_scr[t, F_V, g * 8:(g + 1) * 8, :] for g in range(ng))

        def update(k, acc):
            w = x_scr[t, F_W, pl.ds(k, 1), :]
            kka = x_scr[t, F_KKA, pl.ds(k, 1), :]
            kx = x_scr[t, F_K, pl.ds(k, 1), :]
            r = x_scr[t, F_R, pl.ds(k, 1), :]
            out = []
            for g in range(ng):
                sn = s_ref[k, g * 8:(g + 1) * 8, :] * w - skk[g] * kka + vv[g] * kx
                s_ref[k, g * 8:(g + 1) * 8, :] = sn
                out.append(acc[g] + sn * r)
            return tuple(out)

        o = lax.fori_loop(0, HEAD_DIM, update, zero, unroll=SCAN_K_UNROLL)
        for g in range(ng):
            o_scr[t, g * 8:(g + 1) * 8, :] = o[g]
        return carry

    lax.fori_loop(0, tc, step, 0)

    low = lax.broadcasted_iota(jnp.int32, (1, LANES), 1) < HEAD_DIM
    for t in range(0, tc, 2):
        z = jnp.concatenate([o_scr[t], o_scr[t + 1]], axis=0).T
        zr = pltpu.roll(z, HEAD_DIM, axis=1)
        for p in range(RWKV_HEADS // 2):
            ev = slice(2 * p * nb, (2 * p + 1) * nb)
            od = slice((2 * p + 1) * nb, (2 * p + 2) * nb)
            o_ref[t, :, p * LANES:(p + 1) * LANES] = jnp.where(low, z[ev], zr[od])
            o_ref[t + 1, :, p * LANES:(p + 1) * LANES] = jnp.where(low, zr[ev], z[od])


def _rwkv_scan(scan_in, s0):
    t, b, w = scan_in.shape
    tc = min(SCAN_TC, t)
    state = pl.BlockSpec((HEAD_DIM, HEAD_DIM, LANES), lambda i: (0, 0, 0))
    return pl.pallas_call(
        _rwkv_scan_kernel,
        grid=(t // tc,),
        in_specs=[pl.BlockSpec((tc, b, w), lambda i: (i, 0, 0)), state],
        out_specs=[pl.BlockSpec((tc, b, RWKV_WIDTH), lambda i: (i, 0, 0)), state],
        out_shape=[jax.ShapeDtypeStruct((t, b, RWKV_WIDTH), F32),
                   jax.ShapeDtypeStruct((HEAD_DIM, HEAD_DIM, LANES), F32)],
        scratch_shapes=[pltpu.VMEM((tc, 2 * SCAN_PAIRS, HEAD_DIM, LANES), F32), pltpu.VMEM((tc, HEAD_DIM, LANES), F32)],
        compiler_params=_cparams(("arbitrary",)),
        name="rwkv_scan",
    )(scan_in, s0)


def _retention_kernel(z_ref, cos_ref, sin_ref, d_ref, dq_ref, ds_ref, gc_ref, r0_ref, bd_ref,
                      o_ref, r_scr):
    i = pl.program_id(1)

    @pl.when(i == 0)
    def _():
        r_scr[...] = r0_ref[...]

    cos = cos_ref[...]
    sin = sin_ref[...]
    lane = lax.broadcasted_iota(jnp.int32, (1, LANES), 1)
    low = lane < HEAD_DIM
    first_half = (lane % HEAD_DIM) < (HEAD_DIM // 2)
    rr = lax.broadcasted_iota(jnp.int32, (LANES, LANES), 0)
    cc = lax.broadcasted_iota(jnp.int32, (LANES, LANES), 1)
    same_head = (rr < HEAD_DIM) == (cc < HEAD_DIM)
    bd = bd_ref[...]
    w_ = RET_WIDTH

    def rot(x):
        swapped = jnp.where(first_half, pltpu.roll(x, LANES - HEAD_DIM // 2, axis=1), pltpu.roll(x, HEAD_DIM // 2, axis=1))
        return x * cos + swapped * sin

    for p in range(RET_HEADS // 2):
        c0 = p * LANES
        qr = rot(z_ref[0, :, c0:c0 + LANES])
        kr = rot(z_ref[0, :, w_ + c0:w_ + c0 + LANES]) * (HEAD_DIM ** -0.5)
        vb = z_ref[0, :, 2 * w_ + c0:2 * w_ + c0 + LANES].astype(BF16)
        gate = z_ref[0, :, 3 * w_ + c0:3 * w_ + c0 + LANES]
        qb = qr.astype(BF16)
        kb = kr.astype(BF16)
        r_old = r_scr[0, p]
        o = jnp.dot((qr * dq_ref[p]).astype(BF16), r_old.astype(BF16), preferred_element_type=F32)
        for hh in range(2):
            msk = low if hh == 0 else jnp.logical_not(low)
            qm = jnp.where(msk, qb, jnp.zeros_like(qb))
            pmat = (_qk(qm, kb) * d_ref[2 * p + hh]).astype(BF16)
            o = o + jnp.where(msk, jnp.dot(pmat, vb, preferred_element_type=F32), 0.0)
        kd = (kr * ds_ref[p]).astype(BF16)
        kv = lax.dot_general(kd, vb, (((0,), (0,)), ((), ())), preferred_element_type=F32)
        r_scr[0, p] = r_old * gc_ref[p] + jnp.where(same_head, kv, 0.0)
        out = _head_norm(o, bd, RET_GN_EPS) * (gate * _sigmoid(gate))
        o_ref[0, :, c0:c0 + LANES] = out.astype(o_ref.dtype)


def _retention(zt3, cos, sin, dmat, dq, ds, gc, r0, bd, blk):
    b, t, w = zt3.shape
    npair = RET_HEADS // 2
    full = lambda a: pl.BlockSpec(a.shape, lambda i, j: (0,) * a.ndim)
    return pl.pallas_call(
        _retention_kernel,
        grid=(b, t // blk),
        in_specs=[pl.BlockSpec((1, blk, w), lambda i, j: (i, j, 0)),
                  pl.BlockSpec((blk, LANES), lambda i, j: (j, 0)),
                  pl.BlockSpec((blk, LANES), lambda i, j: (j, 0)),
                  full(dmat), full(dq), full(ds), full(gc),
                  pl.BlockSpec((1, npair, LANES, LANES), lambda i, j: (i, 0, 0, 0)),
                  full(bd)],
        out_specs=[pl.BlockSpec((1, blk, RET_WIDTH), lambda i, j: (i, j, 0)),
                   pl.BlockSpec((1, npair, LANES, LANES), lambda i, j: (i, 0, 0, 0))],
        out_shape=[jax.ShapeDtypeStruct((b, t, RET_WIDTH), BF16),
                   jax.ShapeDtypeStruct((b, npair, LANES, LANES), F32)],
        compiler_params=_cparams(("parallel", "arbitrary")),
        name="retention",
    )(zt3, cos, sin, dmat, dq, ds, gc, r0, bd)


def _retention_tables(blk, chunk, t_len, pos0):
    log_g = jnp.log(1.0 - jnp.power(2.0, -5.0 - jnp.arange(RET_HEADS, dtype=F32)))
    idx = jnp.arange(blk)
    ci = idx // chunk
    diff = (idx[:, None] - idx[None, :]).astype(F32)
    expo = jnp.where(ci[:, None] == ci[None, :], jnp.abs(diff), diff)
    dmat = jnp.exp(log_g[:, None, None] * expo[None])
    dmat = jnp.where((ci[:, None] >= ci[None, :])[None], dmat, 0.0)
    lg_lane = jnp.repeat(log_g, HEAD_DIM).reshape(RET_HEADS // 2, 1, LANES)
    tau = idx.astype(F32)[None, :, None]
    dq = jnp.exp(lg_lane * (tau + 1.0))
    ds = jnp.exp(lg_lane * (blk - 1.0 - tau))
    gc = jnp.exp(lg_lane * float(blk))
    half = HEAD_DIM // 2
    inv = 1.0 / (ROPE_BASE ** (jnp.arange(half, dtype=F32) / half))
    ang = (jnp.arange(t_len) + pos0).astype(F32)[:, None] * inv[None, :]
    cos = jnp.tile(jnp.cos(ang), (1, LANES // half))
    sin = jnp.sin(ang)
    sin = jnp.tile(jnp.concatenate([-sin, sin], axis=1), (1, LANES // HEAD_DIM))
    return cos, sin, dmat, dq, ds, gc


def _dense_out_kernel(x_ref, of_ref, orw_ref, gb_ref, oret_ref, pe_ref, lnw_ref, lnb_ref, bd_ref,
                      wo1_ref, wo2_ref, wo3_ref, n2_ref, wg_ref, wu_ref, wd_ref, wpp_ref, pn_ref, wpg_ref, nf_ref,
                      xo_ref, *, final):
    w_ = RWKV_WIDTH
    hn = _head_norm(orw_ref[...], bd_ref[...], RWKV_GN_EPS)
    orw = ((hn * lnw_ref[...] + lnb_ref[...] + gb_ref[:, w_:2 * w_]) * gb_ref[:, 0:w_]).astype(BF16)
    x = x_ref[...]
    x = x + (jnp.dot(of_ref[...], wo1_ref[...], preferred_element_type=F32)
             + jnp.dot(orw, wo2_ref[...], preferred_element_type=F32)
             + jnp.dot(oret_ref[...], wo3_ref[...], preferred_element_type=F32))
    x = x + 0.5 * _swiglu(_rms(x, n2_ref[...]).astype(BF16), wg_ref, wu_ref, wd_ref)
    e = _rms(jnp.dot(pe_ref[...].astype(BF16), wpp_ref[...], preferred_element_type=F32), pn_ref[...])
    x = x + _sigmoid(jnp.dot(x.astype(BF16), wpg_ref[...], preferred_element_type=F32)) * e
    if final:
        x = _rms(x, nf_ref[...])
    xo_ref[...] = x


def _dense_out(x, of, orw, gb, oret, pe, lnw, lnb, bd, wo1, wo2, wo3, n2, wg, wu, wd, wpp, pn, wpg, nf, final,
               orw_tiles_per_seq, layer):
    n, d = x.shape
    tm = DENSE_TM
    row = lambda a: pl.BlockSpec((tm, a.shape[1]), lambda i: (i, 0))
    pe_spec = pl.BlockSpec((None, tm, pe.shape[2]), lambda i: (layer, i, 0))
    if orw_tiles_per_seq:
        tps = orw_tiles_per_seq
        orw_spec = pl.BlockSpec((tm, RWKV_WIDTH), lambda i: (i % tps, i // tps))
    else:
        orw_spec = row(orw)
    consts = (lnw, lnb, bd, wo1, wo2, wo3, n2, wg, wu, wd, wpp, pn, wpg, nf)
    return pl.pallas_call(
        functools.partial(_dense_out_kernel, final=final),
        grid=(n // tm,),
        in_specs=[row(x), row(of), orw_spec, row(gb), row(oret), pe_spec] + [_resident(c.shape) for c in consts],
        out_specs=row(x),
        out_shape=jax.ShapeDtypeStruct((n, d), F32),
        compiler_params=_cparams(("parallel",)),
        name="dense_out",
    )(x, of, orw, gb, oret, pe, *consts)


def _block_diag_ones(width):
    idx = np.arange(width) // HEAD_DIM
    return jnp.asarray(idx[:, None] == idx[None, :], F32)


def _group(x, pe_all, hist, lw, tables, layer, final):
    (n1, wg1, wu1, wd1, nm, win, bfp, mu, w0, a0, k_k, k_a, r_k, wup, aup, gup, lnw, lnb,
     wo1, wo2, wo3, n2, wg2, wu2, wd2, wpp, pn, wpg, nf) = lw
    b, t, d = x.shape
    n = b * t
    bd256, bd128 = _block_diag_ones(RWKV_WIDTH), _block_diag_ones(LANES)
    x2, q, k, v, fg, lf, zr, zt = _dense_in(x.reshape(n, d), n1, wg1, wu1, wd1, nm, win, bfp)
    r3 = lambda a: a.reshape(b, t, a.shape[-1])
    lf3 = r3(lf)
    logf = lf3[:, :, :FOX_HEADS]
    zr3 = r3(zr)
    if hist is None:
        lft = logf.transpose(0, 2, 1)
        ccol, crow = _cumsum(lf3, lft)
        of = _fox_prompt(r3(q), r3(k), r3(v), r3(fg), ccol, crow)
        prev = jnp.zeros((b, 1, RW_PAD), F32)
        s0 = jnp.zeros((HEAD_DIM, HEAD_DIM, RWKV_HEADS * b), F32)
        r0 = jnp.zeros((b, RET_HEADS // 2, LANES, LANES), F32)
        blk = RET_BLOCK
    else:
        kc, vc, clf, s0, prev, r0 = hist
        past = clf.shape[1]
        lf_all = jnp.concatenate([jnp.pad(clf, ((0, 0), (0, 0), (0, LANES - FOX_HEADS))), lf3], axis=1)
        lft = lf_all[:, :, :FOX_HEADS].transpose(0, 2, 1)
        ccol, crow = _cumsum(lf_all, lft)
        of = _fox_sample(r3(q), r3(k), r3(v), r3(fg), kc, vc, layer, ccol, crow)
        blk = t
    cos, sin, dmat, dq, ds, gc = tables
    scan_in, post_in = _rwkv_pre(zr3, prev, mu, w0, a0, k_k, k_a, r_k, wup, aup, gup, bd256)
    o_tm, s_fin = _rwkv_scan(scan_in.reshape(t, b, 6 * RWKV_WIDTH), s0)
    if t % DENSE_TM == 0:
        orw, tiles_per_seq = o_tm.reshape(t, b * RWKV_WIDTH), t // DENSE_TM
    else:
        orw, tiles_per_seq = o_tm.transpose(1, 0, 2).reshape(n, RWKV_WIDTH), 0
    oret, r_fin = _retention(r3(zt), cos, sin, dmat, dq, ds, gc, r0, bd128, blk)
    xo = _dense_out(x2, of.reshape(n, FOX_WIDTH), orw, post_in.reshape(n, 2 * RWKV_WIDTH), oret.reshape(n, RET_WIDTH),
                    pe_all.reshape(pe_all.shape[0], n, pe_all.shape[-1]), lnw, lnb, bd256, wo1, wo2, wo3, n2, wg2, wu2,
                    wd2, wpp, pn, wpg, nf, final, tiles_per_seq, layer)
    s_new = s_fin.reshape(HEAD_DIM, HEAD_DIM, RWKV_HEADS, b).transpose(3, 2, 1, 0)
    r_new = jnp.stack([r_fin[:, hh // 2, (hh % 2) * HEAD_DIM:(hh % 2 + 1) * HEAD_DIM,
                             (hh % 2) * HEAD_DIM:(hh % 2 + 1) * HEAD_DIM] for hh in range(RET_HEADS)], axis=1)
    state = (r3(k).reshape(b, t, FOX_HEADS, HEAD_DIM), r3(v).reshape(b, t, FOX_HEADS, HEAD_DIM), logf,
             s_new, zr3[:, t - 1:t, :], r_new)
    return xo.reshape(b, t, d), state


def kernel(x_prompt, x_sample, cache_fox_k, cache_fox_v, cache_fox_logf, state_rwkv, state_rwkv_shift, state_ret,
           p_prompt, p_sample, norm_ffn1, w_ffn1_gate, w_ffn1_up, w_ffn1_down, norm_mix, w_in, fox_forget_bias,
           rwkv_mu, rwkv_w0, rwkv_w_up, rwkv_a0, rwkv_a_up, rwkv_g_up, rwkv_k_k, rwkv_k_a, rwkv_r_k, rwkv_ln_w,
           rwkv_ln_b, w_out, norm_ffn2, w_ffn2_gate, w_ffn2_up, w_ffn2_down, w_ple_proj, ple_norm, w_ple_gate,
           norm_final):
    depth, d, _ = w_in.shape
    bp, tp, _ = x_prompt.shape
    bs, ts, _ = x_sample.shape
    past = cache_fox_k.shape[2]
    assert RWKV_HEADS * bp == LANES and RWKV_HEADS * bs == LANES, "rwkv scan puts (head, batch) on the 128 lanes"
    assert tp % RET_BLOCK == 0 and tp % FOX_BQ == 0 and (bp * tp) % DENSE_TM == 0 and (bs * ts) % DENSE_TM == 0

    fw, rw = FOX_WIDTH, RWKV_WIDTH
    o_f = 3 * fw
    o_g = o_f + FOX_HEADS
    o_rw = o_g + fw
    o_ret = o_rw + 3 * rw + 3 * RWKV_RANK
    zpad = lambda w: jnp.zeros((depth, d, w), F32)
    r_ = RWKV_RANK
    rw_cols = 3 * rw + 3 * r_

    def regroup(a):
        parts = [a[..., 0:rw], a[..., rw + r_:2 * rw + r_], a[..., 2 * rw + r_:3 * rw + r_], a[..., rw:rw + r_],
                 a[..., 3 * rw + r_:rw_cols], jnp.zeros(a.shape[:-1] + (RW_PAD - rw_cols,), a.dtype)]
        return jnp.concatenate(parts, axis=-1)

    def ungroup(a):
        parts = [a[..., 0:rw], a[..., 3 * rw:3 * rw + r_], a[..., rw:3 * rw], a[..., 3 * rw + r_:rw_cols]]
        return jnp.concatenate(parts, axis=-1)

    win_all = jnp.concatenate([w_in[:, :, 0:o_f], w_in[:, :, o_g:o_rw], w_in[:, :, o_f:o_g], zpad(LANES - FOX_HEADS),
                               regroup(w_in[:, :, o_rw:o_ret]), w_in[:, :, o_ret:]], axis=2).astype(BF16)
    bfp = jnp.pad(fox_forget_bias, ((0, 0), (0, LANES - FOX_HEADS)))[:, None, :]
    mu_p = regroup(rwkv_mu)[:, None, :]
    lowrank = lambda w, slot: jnp.pad(w, ((0, 0), (slot * RWKV_RANK, LANES - (slot + 1) * RWKV_RANK), (0, 0))).astype(BF16)
    wup, aup, gup = lowrank(rwkv_w_up, 0), lowrank(rwkv_a_up, 1), lowrank(rwkv_g_up, 2)
    vec = lambda a: a.reshape(depth, 1, -1)
    bf = lambda a: a.astype(BF16)
    wg1, wu1, wd1 = bf(w_ffn1_gate), bf(w_ffn1_up), bf(w_ffn1_down)
    wg2, wu2, wd2 = bf(w_ffn2_gate), bf(w_ffn2_up), bf(w_ffn2_down)
    wo = bf(w_out)
    wpp, wpg = bf(w_ple_proj), bf(w_ple_gate)
    nf = norm_final.reshape(1, -1)

    def layer_weights(l):
        return (vec(norm_ffn1)[l], wg1[l], wu1[l], wd1[l], vec(norm_mix)[l], win_all[l], bfp[l], mu_p[l],
                vec(rwkv_w0)[l], vec(rwkv_a0)[l], vec(rwkv_k_k)[l], vec(rwkv_k_a)[l], vec(rwkv_r_k)[l],
                wup[l], aup[l], gup[l], vec(rwkv_ln_w)[l], vec(rwkv_ln_b)[l],
                wo[l, 0:fw], wo[l, fw:fw + rw], wo[l, fw + rw:], vec(norm_ffn2)[l], wg2[l], wu2[l], wd2[l],
                wpp[l], vec(ple_norm)[l], wpg[l], nf)

    kc = cache_fox_k.reshape(depth, bs, past, fw)
    vc = cache_fox_v.reshape(depth, bs, past, fw)
    s0_all = state_rwkv.transpose(0, 4, 3, 2, 1).reshape(depth, HEAD_DIM, HEAD_DIM, RWKV_HEADS * bs)
    prev_all = regroup(state_rwkv_shift)
    eye2 = jnp.eye(2, dtype=F32)
    r0_all = jnp.einsum('lbpqde,qr->lbpqdre', state_ret.reshape(depth, bs, RET_HEADS // 2, 2, HEAD_DIM, HEAD_DIM),
                        eye2).reshape(depth, bs, RET_HEADS // 2, LANES, LANES)

    tab_p = _retention_tables(RET_BLOCK, RET_CHUNK, tp, 0)
    tab_s = _retention_tables(ts, ts, ts, past)

    xp, xs = x_prompt, x_sample
    p_states, s_states = [], []
    for l in range(depth):
        lw = layer_weights(l)
        final = l == depth - 1
        xp, sp = _group(xp, p_prompt, None, lw, tab_p, l, final)
        hist = (kc, vc, cache_fox_logf[l], s0_all[l], prev_all[l], r0_all[l])
        xs, ss = _group(xs, p_sample, hist, lw, tab_s, l, final)
        p_states.append(sp)
        s_states.append(ss)

    def stacked(states):
        cols = [jnp.stack([s[i] for s in states]) for i in range(6)]
        cols[4] = ungroup(cols[4])
        return cols

    return (xp, xs, *stacked(p_states), *stacked(s_states))
```

```python
import functools

import jax
import jax.numpy as jnp
import numpy as np
from jax import lax
from jax.experimental import pallas as pl
from jax.experimental.pallas import tpu as pltpu

F32 = jnp.float32
BF16 = jnp.bfloat16

HEAD_DIM = 64
FOX_HEADS = 8
RWKV_HEADS = 4
RET_HEADS = 4
FOX_WIDTH = FOX_HEADS * HEAD_DIM
RWKV_WIDTH = RWKV_HEADS * HEAD_DIM
RET_WIDTH = RET_HEADS * HEAD_DIM
RWKV_RANK = 32
RET_CHUNK = 64
ROPE_BASE = 10000.0
NORM_EPS = 1e-6
RWKV_GN_EPS = 64e-5
RET_GN_EPS = 1e-6

LANES = 128
VMEM_LIMIT_BYTES = 58 * 1024 * 1024
DENSE_TM = 256
FFN_CHUNK = 1408
FOX_BQ = 256
CUM_BLK = 256
RWKV_TT = 256
SCAN_TC = 8
SCAN_K_UNROLL = 8
SCAN_PAIRS = 3
F_R, F_W, F_K, F_V, F_KK, F_KKA = range(6)
RET_BLOCK = 256

C_Q, C_K, C_V, C_G, C_F, C_RW, C_RET, C_END = 0, 512, 1024, 1536, 2048, 2176, 3072, 4096
RW_PAD = C_RET - C_RW


def _cparams(sem):
    return pltpu.CompilerParams(dimension_semantics=sem, vmem_limit_bytes=VMEM_LIMIT_BYTES)


def _resident(shape):
    nd = len(shape)
    return pl.BlockSpec(shape, lambda *_: (0,) * nd, pipeline_mode=pl.Buffered(1))


def _rms(x, g):
    return x * lax.rsqrt(jnp.mean(x * x, axis=-1, keepdims=True) + NORM_EPS) * g


def _sigmoid(x):
    return 1.0 / (1.0 + jnp.exp(-x))


def _log1pexp_neg_abs(x):
    return jnp.log(1.0 + jnp.exp(-jnp.abs(x)))


def _split2(x):
    hi = x.astype(BF16).astype(F32)
    lo = (x - hi).astype(BF16).astype(F32)
    return hi, lo


def _seg_sum(x, bd):
    hi, lo = _split2(x)
    return jnp.dot(hi, bd, preferred_element_type=F32) + jnp.dot(lo, bd, preferred_element_type=F32)


def _head_norm(o, bd, eps):
    mu = _seg_sum(o, bd) * (1.0 / HEAD_DIM)
    d = o - mu
    var = _seg_sum(d * d, bd) * (1.0 / HEAD_DIM)
    return d * lax.rsqrt(var + eps)


def _swiglu(h, wg_ref, wu_ref, wd_ref):
    d_ff = wg_ref.shape[1]
    acc = None
    for c0 in range(0, d_ff, FFN_CHUNK):
        g = jnp.dot(h, wg_ref[:, c0:c0 + FFN_CHUNK], preferred_element_type=F32)
        u = jnp.dot(h, wu_ref[:, c0:c0 + FFN_CHUNK], preferred_element_type=F32)
        a = (g * _sigmoid(g) * u).astype(BF16)
        d = jnp.dot(a, wd_ref[c0:c0 + FFN_CHUNK, :], preferred_element_type=F32)
        acc = d if acc is None else acc + d
    return acc


def _dense_in_kernel(x_ref, n1_ref, wg_ref, wu_ref, wd_ref, nm_ref, win_ref, wkvt_ref, bf_ref,
                     xo_ref, q_ref, k_ref, v_ref, g_ref, lf_ref, zr_ref, zt_ref, *, kv_transposed):
    x = x_ref[...]
    x = x + 0.5 * _swiglu(_rms(x, n1_ref[...]).astype(BF16), wg_ref, wu_ref, wd_ref)
    xo_ref[...] = x
    h = _rms(x, nm_ref[...]).astype(BF16)
    if kv_transposed:
        zq = jnp.dot(h, win_ref[:, C_Q:C_K], preferred_element_type=F32)
        kvt = lax.dot_general(wkvt_ref[...], h, (((1,), (1,)), ((), ())), preferred_element_type=F32)
        k_ref[0] = kvt[:FOX_WIDTH]
        v_ref[0] = kvt[FOX_WIDTH:]
        z = jnp.dot(h, win_ref[:, C_G:C_END], preferred_element_type=F32)
        off = C_G
    else:
        z = jnp.dot(h, win_ref[...], preferred_element_type=F32)
        zq = z[:, C_Q:C_K]
        k_ref[...] = z[:, C_K:C_V]
        v_ref[...] = z[:, C_V:C_G]
        off = 0
    q_ref[...] = (zq * (HEAD_DIM ** -0.5)).astype(BF16)
    g_ref[...] = z[:, C_G - off:C_F - off]
    f = z[:, C_F - off:C_RW - off] + bf_ref[...]
    lf_ref[...] = jnp.minimum(f, 0.0) - _log1pexp_neg_abs(f)
    zr_ref[...] = z[:, C_RW - off:C_RET - off]
    zt_ref[...] = z[:, C_RET - off:C_END - off]


def _dense_in(x, n1, wg, wu, wd, nm, win, wkvt, bfp, tiles_per_seq):
    n, d = x.shape
    tm = DENSE_TM
    row = lambda w: pl.BlockSpec((tm, w), lambda i: (i, 0))
    outs = [(d, F32), (FOX_WIDTH, BF16), (FOX_WIDTH, F32), (FOX_WIDTH, F32), (FOX_WIDTH, F32),
            (LANES, F32), (RW_PAD, F32), (C_END - C_RET, F32)]
    out_specs = [row(w) for w, _ in outs]
    out_shape = [jax.ShapeDtypeStruct((n, w), dt) for w, dt in outs]
    if tiles_per_seq:
        tps = tiles_per_seq
        for j in (2, 3):
            out_specs[j] = pl.BlockSpec((1, FOX_WIDTH, tm), lambda i: (i // tps, 0, i % tps))
            out_shape[j] = jax.ShapeDtypeStruct((n // (tps * tm), FOX_WIDTH, tps * tm), F32)
    return pl.pallas_call(
        functools.partial(_dense_in_kernel, kv_transposed=bool(tiles_per_seq)),
        grid=(n // tm,),
        in_specs=[row(d), _resident(n1.shape), _resident(wg.shape), _resident(wu.shape), _resident(wd.shape),
                  _resident(nm.shape), _resident(win.shape), _resident(wkvt.shape), _resident(bfp.shape)],
        out_specs=out_specs,
        out_shape=out_shape,
        compiler_params=_cparams(("parallel",)),
        name="dense_in",
    )(x, n1, wg, wu, wd, nm, win, wkvt, bfp)


def _cumsum_kernel(lf_ref, lft_ref, ccol_ref, crow_ref, *, t_len):
    nh = lft_ref.shape[1]
    r = lax.broadcasted_iota(jnp.int32, (CUM_BLK, CUM_BLK), 0)
    c = lax.broadcasted_iota(jnp.int32, (CUM_BLK, CUM_BLK), 1)
    tri_l = (r >= c).astype(F32)
    tri_u = (r <= c).astype(F32)
    carry_c = jnp.zeros((1, LANES), F32)
    carry_r = jnp.zeros((nh, 1), F32)
    for s in range(0, t_len, CUM_BLK):
        n = min(CUM_BLK, t_len - s)
        x = lf_ref[0, s:s + n, :]
        hi = x.astype(BF16).astype(F32)
        r1 = x - hi
        mid = r1.astype(BF16).astype(F32)
        lo = r1 - mid
        tl = tri_l[:n, :n]
        cc = (jnp.dot(tl, hi, preferred_element_type=F32) + jnp.dot(tl, mid, preferred_element_type=F32)
              + jnp.dot(tl, lo, preferred_element_type=F32)) + carry_c
        ccol_ref[0, s:s + n, :] = cc
        carry_c = cc[n - 1:n, :]
        y = lft_ref[0, :, s:s + n]
        hi = y.astype(BF16).astype(F32)
        r1 = y - hi
        mid = r1.astype(BF16).astype(F32)
        lo = r1 - mid
        tu = tri_u[:n, :n]
        cr = (jnp.dot(hi, tu, preferred_element_type=F32) + jnp.dot(mid, tu, preferred_element_type=F32)
              + jnp.dot(lo, tu, preferred_element_type=F32)) + carry_r
        crow_ref[0, :, s:s + n] = cr
        carry_r = cr[:, n - 1:n]


def _cumsum(lf3, lft):
    b, t, _ = lf3.shape
    nh = lft.shape[1]
    return pl.pallas_call(
        functools.partial(_cumsum_kernel, t_len=t),
        grid=(b,),
        in_specs=[pl.BlockSpec((1, t, LANES), lambda i: (i, 0, 0)), pl.BlockSpec((1, nh, t), lambda i: (i, 0, 0))],
        out_specs=[pl.BlockSpec((1, t, LANES), lambda i: (i, 0, 0)), pl.BlockSpec((1, nh, t), lambda i: (i, 0, 0))],
        out_shape=[jax.ShapeDtypeStruct((b, t, LANES), F32), jax.ShapeDtypeStruct((b, nh, t), F32)],
        compiler_params=_cparams(("parallel",)),
        name="logf_cumsum",
    )(lf3, lft)


def _qk(q, k):
    return lax.dot_general(q, k, (((1,), (1,)), ((), ())), preferred_element_type=F32)


def _softmax_pv(s_list, v_list, v_transposed):
    m = None
    for s in s_list:
        mi = jnp.max(s, axis=1, keepdims=True)
        m = mi if m is None else jnp.maximum(m, mi)
    l = None
    o = None
    for s, v, vt in zip(s_list, v_list, v_transposed):
        p = jnp.exp(s - m)
        li = jnp.sum(p, axis=1, keepdims=True)
        oi = _qk(p.astype(BF16), v) if vt else jnp.dot(p.astype(BF16), v, preferred_element_type=F32)
        l = li if l is None else l + li
        o = oi if o is None else o + oi
    return o / l


def _fox_prompt_kernel(q_ref, k_ref, v_ref, g_ref, ccol_ref, crow_ref, o_ref, *, t_len):
    pair = pl.program_id(1)
    lane = lax.broadcasted_iota(jnp.int32, (1, LANES), 1)
    low = lane < HEAD_DIM
    kb = k_ref[0].astype(BF16)
    vb = v_ref[0].astype(BF16)
    rr = lax.broadcasted_iota(jnp.int32, (FOX_BQ, FOX_BQ), 0)
    cc = lax.broadcasted_iota(jnp.int32, (FOX_BQ, FOX_BQ), 1)
    causal = rr >= cc
    for s0 in range(0, t_len, FOX_BQ):
        e = s0 + FOX_BQ
        q = q_ref[0, s0:e, :]
        ccol = ccol_ref[0, s0:e, :]
        res = []
        for hh in range(2):
            h = 2 * pair + hh
            msk = low if hh == 0 else jnp.logical_not(low)
            qm = jnp.where(msk, q, jnp.zeros_like(q))
            cq = jnp.sum(jnp.where(lane == h, ccol, 0.0), axis=1, keepdims=True)
            ck = crow_ref[0, pl.ds(h, 1), :]
            sd = jnp.dot(qm, kb[:, s0:e], preferred_element_type=F32) + cq - ck[:, s0:e]
            sd = jnp.where(causal, sd, -jnp.inf)
            if s0 > 0:
                sp = jnp.dot(qm, kb[:, :s0], preferred_element_type=F32) + cq - ck[:, :s0]
                res.append(_softmax_pv([sp, sd], [vb[:, :s0], vb[:, s0:e]], [True, True]))
            else:
                res.append(_softmax_pv([sd], [vb[:, s0:e]], [True]))
        o = jnp.where(low, res[0], res[1]) * _sigmoid(g_ref[0, s0:e, :])
        o_ref[0, s0:e, :] = o.astype(o_ref.dtype)


def _fox_prompt(q3, kt, vt, g3, ccol, crow):
    b, t, w = q3.shape
    npair = w // LANES
    nh = crow.shape[1]
    blk = lambda: pl.BlockSpec((1, t, LANES), lambda i, p: (i, 0, p))
    blk_t = lambda: pl.BlockSpec((1, LANES, t), lambda i, p: (i, p, 0))
    return pl.pallas_call(
        functools.partial(_fox_prompt_kernel, t_len=t),
        grid=(b, npair),
        in_specs=[blk(), blk_t(), blk_t(), blk(),
                  pl.BlockSpec((1, t, LANES), lambda i, p: (i, 0, 0)),
                  pl.BlockSpec((1, nh, t), lambda i, p: (i, 0, 0))],
        out_specs=blk(),
        out_shape=jax.ShapeDtypeStruct((b, t, w), BF16),
        compiler_params=_cparams(("parallel", "parallel")),
        name="fox_prompt",
    )(q3, kt, vt, g3, ccol, crow)


def _fox_sample_kernel(q_ref, kn_ref, vn_ref, g_ref, kc_ref, vc_ref, ccol_ref, crow_ref, o_ref, *, past, n_new):
    pair = pl.program_id(1)
    lane = lax.broadcasted_iota(jnp.int32, (1, LANES), 1)
    low = lane < HEAD_DIM
    kc = kc_ref[0, 0].reshape(LANES, past).astype(BF16)
    vc = vc_ref[0, 0].reshape(LANES, past).astype(BF16)
    kn = kn_ref[0].astype(BF16)
    vn = vn_ref[0].astype(BF16)
    q = q_ref[0]
    ccol = ccol_ref[0, past:past + n_new, :]
    rr = lax.broadcasted_iota(jnp.int32, (n_new, n_new), 0)
    cc = lax.broadcasted_iota(jnp.int32, (n_new, n_new), 1)
    causal = rr >= cc
    res = []
    for hh in range(2):
        h = 2 * pair + hh
        msk = low if hh == 0 else jnp.logical_not(low)
        qm = jnp.where(msk, q, jnp.zeros_like(q))
        cq = jnp.sum(jnp.where(lane == h, ccol, 0.0), axis=1, keepdims=True)
        ck = crow_ref[0, pl.ds(h, 1), :]
        sc = jnp.dot(qm, kc, preferred_element_type=F32) + cq - ck[:, :past]
        sn = _qk(qm, kn) + cq - ck[:, past:past + n_new]
        sn = jnp.where(causal, sn, -jnp.inf)
        res.append(_softmax_pv([sc, sn], [vc, vn], [True, False]))
    o = jnp.where(low, res[0], res[1]) * _sigmoid(g_ref[0])
    o_ref[0] = o.astype(o_ref.dtype)


def _fox_sample(q3, k3, v3, g3, kc, vc, layer, ccol, crow):
    b, n_new, w = q3.shape
    past = kc.shape[4]
    npair = w // LANES
    nh = crow.shape[1]
    tot = past + n_new
    blk = lambda: pl.BlockSpec((1, n_new, LANES), lambda i, p: (i, 0, p))
    cache = lambda: pl.BlockSpec((1, 1, 2, HEAD_DIM, past), lambda i, p: (layer, i, p, 0, 0))
    return pl.pallas_call(
        functools.partial(_fox_sample_kernel, past=past, n_new=n_new),
        grid=(b, npair),
        in_specs=[blk(), blk(), blk(), blk(), cache(), cache(),
                  pl.BlockSpec((1, tot, LANES), lambda i, p: (i, 0, 0)),
                  pl.BlockSpec((1, nh, tot), lambda i, p: (i, 0, 0))],
        out_specs=blk(),
        out_shape=jax.ShapeDtypeStruct((b, n_new, w), BF16),
        compiler_params=_cparams(("parallel", "parallel")),
        name="fox_sample",
    )(q3, k3, v3, g3, kc, vc, ccol, crow)


def _rwkv_pre_kernel(z_ref, zp_ref, prev_ref, mu_ref, w0_ref, a0_ref, kk_ref, ka_ref, rk_ref,
                     wup_ref, aup_ref, gup_ref, bd_ref, scan_ref, post_ref):
    i = pl.program_id(1)
    z = z_ref[0]
    tt = z.shape[0]
    p_in = jnp.where(i == 0, prev_ref[0], zp_ref[0, 7:8, :])
    row = lax.broadcasted_iota(jnp.int32, (tt, 1), 0)
    shifted = jnp.where(row == 0, p_in, pltpu.roll(z, 1, axis=0))
    zs = z + mu_ref[...] * (shifted - z)
    w_ = RWKV_WIDTH
    r, k, v, low = zs[:, 0:w_], zs[:, w_:2 * w_], zs[:, 2 * w_:3 * w_], zs[:, 3 * w_:]
    bd = bd_ref[...]
    y = w0_ref[...] + jnp.dot(jnp.tanh(low).astype(BF16), wup_ref[...], preferred_element_type=F32)
    sp = jnp.maximum(-y, 0.0) + _log1pexp_neg_abs(y)
    decay = jnp.exp(-jnp.exp(-sp - 0.5))
    a = _sigmoid(a0_ref[...] + jnp.dot(low.astype(BF16), aup_ref[...], preferred_element_type=F32))
    g = jnp.dot(_sigmoid(low).astype(BF16), gup_ref[...], preferred_element_type=F32)
    kk = k * kk_ref[...]
    kk = kk * lax.rsqrt(jnp.maximum(_seg_sum(kk * kk, bd), 1e-24))
    k2 = k * (1.0 + (a - 1.0) * ka_ref[...])
    bonus = _seg_sum(r * k2 * rk_ref[...], bd) * v
    fields = {F_R: r, F_W: decay, F_K: k2, F_V: v, F_KK: kk, F_KKA: kk * a}
    per_head = 2 * SCAN_PAIRS * HEAD_DIM
    for h in range(RWKV_HEADS):
        for f, val in fields.items():
            c0 = h * per_head + f * HEAD_DIM
            scan_ref[:, c0:c0 + HEAD_DIM] = val[:, h * HEAD_DIM:(h + 1) * HEAD_DIM]
    post_ref[0, :, 0:w_] = g
    post_ref[0, :, w_:2 * w_] = bonus


def _rwkv_pre(zr3, prev, mu, w0, a0, k_k, k_a, r_k, wup, aup, gup, bd):
    b, t, w = zr3.shape
    tt = min(RWKV_TT, t)
    full = lambda a: pl.BlockSpec(a.shape, lambda i, j: (0,) * a.ndim)
    return pl.pallas_call(
        _rwkv_pre_kernel,
        grid=(b, t // tt),
        in_specs=[pl.BlockSpec((1, tt, w), lambda i, j: (i, j, 0)),
                  pl.BlockSpec((1, 8, w), lambda i, j: (i, jnp.maximum(j * (tt // 8) - 1, 0), 0)),
                  pl.BlockSpec((1, 1, w), lambda i, j: (i, 0, 0)),
                  full(mu), full(w0), full(a0), full(k_k), full(k_a), full(r_k), full(wup), full(aup), full(gup),
                  full(bd)],
        out_specs=[pl.BlockSpec((tt, 6 * RWKV_WIDTH), lambda i, j: (j, i)),
                   pl.BlockSpec((1, tt, 2 * RWKV_WIDTH), lambda i, j: (i, j, 0))],
        out_shape=[jax.ShapeDtypeStruct((t, b * 6 * RWKV_WIDTH), F32), jax.ShapeDtypeStruct((b, t, 2 * RWKV_WIDTH), F32)],
        compiler_params=_cparams(("parallel", "parallel")),
        name="rwkv_pre",
    )(zr3, zr3, prev, mu, w0, a0, k_k, k_a, r_k, wup, aup, gup, bd)


def _rwkv_scan_kernel(*refs):
    x_ref, s0_ref, o_ref, s_ref, x_scr, o_scr = refs
    i = pl.program_id(0)
    tc, nb = x_ref.shape[0], x_ref.shape[1]
    ng = HEAD_DIM // 8

    @pl.when(i == 0)
    def _():
        s_ref[...] = s0_ref[...]

    for t in range(tc):
        for fp in range(SCAN_PAIRS):
            cols = [(h * SCAN_PAIRS + fp) * LANES for h in range(RWKV_HEADS)]
            y = jnp.concatenate([x_ref[t, :, c:c + LANES] for c in cols], axis=0)
            yt = y.T
            x_scr[t, 2 * fp] = yt[0:HEAD_DIM]
            x_scr[t, 2 * fp + 1] = yt[HEAD_DIM:2 * HEAD_DIM]

    def step(t, carry):
        def dot_kk(k, acc):
            kk = x_scr[t, F_KK, pl.ds(k, 1), :]
            return tuple(acc[g] + s_ref[k, g * 8:(g + 1) * 8, :] * kk for g in range(ng))

        zero = tuple(jnp.zeros((8, LANES), F32) for _ in range(ng))
        skk = lax.fori_loop(0, HEAD_DIM, dot_kk, zero, unroll=SCAN_K_UNROLL)
        vv = tuple(x_scr[t, F_V, g * 8:(g + 1) * 8, :] for g in range(ng))

        def update(k, acc):
            w = x_scr[t, F_W, pl.ds(k, 1), :]
            kka = x_scr[t, F_KKA, pl.ds(k, 1), :]
            kx = x_scr[t, F_K, pl.ds(k, 1), :]
            r = x_scr[t, F_R, pl.ds(k, 1), :]
            out = []
            for g in range(ng):
                sn = s_ref[k, g * 8:(g + 1) * 8, :] * w - skk[g] * kka + vv[g] * kx
                s_ref[k, g * 8:(g + 1) * 8, :] = sn
                out.append(acc[g] + sn * r)
            return tuple(out)

        o = lax.fori_loop(0, HEAD_DIM, update, zero, unroll=SCAN_K_UNROLL)
        for g in range(ng):
            o_scr[t, g * 8:(g + 1) * 8, :] = o[g]
        return carry

    lax.fori_loop(0, tc, step, 0)

    low = lax.broadcasted_iota(jnp.int32, (1, LANES), 1) < HEAD_DIM
    for t in range(0, tc, 2):
        z = jnp.concatenate([o_scr[t], o_scr[t + 1]], axis=0).T
        zr = pltpu.roll(z, HEAD_DIM, axis=1)
        for p in range(RWKV_HEADS // 2):
            ev = slice(2 * p * nb, (2 * p + 1) * nb)
            od = slice((2 * p + 1) * nb, (2 * p + 2) * nb)
            o_ref[t, :, p * LANES:(p + 1) * LANES] = jnp.where(low, z[ev], zr[od])
            o_ref[t + 1, :, p * LANES:(p + 1) * LANES] = jnp.where(low, zr[ev], z[od])


def _rwkv_scan(scan_in, s0):
    t, b, w = scan_in.shape
    tc = min(SCAN_TC, t)
    state = pl.BlockSpec((HEAD_DIM, HEAD_DIM, LANES), lambda i: (0, 0, 0))
    return pl.pallas_call(
        _rwkv_scan_kernel,
        grid=(t // tc,),
        in_specs=[pl.BlockSpec((tc, b, w), lambda i: (i, 0, 0)), state],
        out_specs=[pl.BlockSpec((tc, b, RWKV_WIDTH), lambda i: (i, 0, 0)), state],
        out_shape=[jax.ShapeDtypeStruct((t, b, RWKV_WIDTH), F32),
                   jax.ShapeDtypeStruct((HEAD_DIM, HEAD_DIM, LANES), F32)],
        scratch_shapes=[pltpu.VMEM((tc, 2 * SCAN_PAIRS, HEAD_DIM, LANES), F32), pltpu.VMEM((tc, HEAD_DIM, LANES), F32)],
        compiler_params=_cparams(("arbitrary",)),
        name="rwkv_scan",
    )(scan_in, s0)


def _retention_kernel(z_ref, cos_ref, sin_ref, d_ref, dq_ref, ds_ref, gc_ref, r0_ref, bd_ref,
                      o_ref, r_scr):
    i = pl.program_id(1)

    @pl.when(i == 0)
    def _():
        r_scr[...] = r0_ref[...]

    cos = cos_ref[...]
    sin = sin_ref[...]
    lane = lax.broadcasted_iota(jnp.int32, (1, LANES), 1)
    low = lane < HEAD_DIM
    first_half = (lane % HEAD_DIM) < (HEAD_DIM // 2)
    rr = lax.broadcasted_iota(jnp.int32, (LANES, LANES), 0)
    cc = lax.broadcasted_iota(jnp.int32, (LANES, LANES), 1)
    same_head = (rr < HEAD_DIM) == (cc < HEAD_DIM)
    bd = bd_ref[...]
    w_ = RET_WIDTH

    def rot(x):
        swapped = jnp.where(first_half, pltpu.roll(x, LANES - HEAD_DIM // 2, axis=1), pltpu.roll(x, HEAD_DIM // 2, axis=1))
        return x * cos + swapped * sin

    for p in range(RET_HEADS // 2):
        c0 = p * LANES
        qr = rot(z_ref[0, :, c0:c0 + LANES])
        kr = rot(z_ref[0, :, w_ + c0:w_ + c0 + LANES]) * (HEAD_DIM ** -0.5)
        vb = z_ref[0, :, 2 * w_ + c0:2 * w_ + c0 + LANES].astype(BF16)
        gate = z_ref[0, :, 3 * w_ + c0:3 * w_ + c0 + LANES]
        qb = qr.astype(BF16)
        kb = kr.astype(BF16)
        r_old = r_scr[0, p]
        o = jnp.dot((qr * dq_ref[p]).astype(BF16), r_old.astype(BF16), preferred_element_type=F32)
        for hh in range(2):
            msk = low if hh == 0 else jnp.logical_not(low)
            qm = jnp.where(msk, qb, jnp.zeros_like(qb))
            pmat = (_qk(qm, kb) * d_ref[2 * p + hh]).astype(BF16)
            o = o + jnp.where(msk, jnp.dot(pmat, vb, preferred_element_type=F32), 0.0)
        kd = (kr * ds_ref[p]).astype(BF16)
        kv = lax.dot_general(kd, vb, (((0,), (0,)), ((), ())), preferred_element_type=F32)
        r_scr[0, p] = r_old * gc_ref[p] + jnp.where(same_head, kv, 0.0)
        out = _head_norm(o, bd, RET_GN_EPS) * (gate * _sigmoid(gate))
        o_ref[0, :, c0:c0 + LANES] = out.astype(o_ref.dtype)


def _retention(zt3, cos, sin, dmat, dq, ds, gc, r0, bd, blk):
    b, t, w = zt3.shape
    npair = RET_HEADS // 2
    full = lambda a: pl.BlockSpec(a.shape, lambda i, j: (0,) * a.ndim)
    return pl.pallas_call(
        _retention_kernel,
        grid=(b, t // blk),
        in_specs=[pl.BlockSpec((1, blk, w), lambda i, j: (i, j, 0)),
                  pl.BlockSpec((blk, LANES), lambda i, j: (j, 0)),
                  pl.BlockSpec((blk, LANES), lambda i, j: (j, 0)),
                  full(dmat), full(dq), full(ds), full(gc),
                  pl.BlockSpec((1, npair, LANES, LANES), lambda i, j: (i, 0, 0, 0)),
                  full(bd)],
        out_specs=[pl.BlockSpec((1, blk, RET_WIDTH), lambda i, j: (i, j, 0)),
                   pl.BlockSpec((1, npair, LANES, LANES), lambda i, j: (i, 0, 0, 0))],
        out_shape=[jax.ShapeDtypeStruct((b, t, RET_WIDTH), BF16),
                   jax.ShapeDtypeStruct((b, npair, LANES, LANES), F32)],
        compiler_params=_cparams(("parallel", "arbitrary")),
        name="retention",
    )(zt3, cos, sin, dmat, dq, ds, gc, r0, bd)


def _retention_tables(blk, chunk, t_len, pos0):
    log_g = jnp.log(1.0 - jnp.power(2.0, -5.0 - jnp.arange(RET_HEADS, dtype=F32)))
    idx = jnp.arange(blk)
    ci = idx // chunk
    diff = (idx[:, None] - idx[None, :]).astype(F32)
    expo = jnp.where(ci[:, None] == ci[None, :], jnp.abs(diff), diff)
    dmat = jnp.exp(log_g[:, None, None] * expo[None])
    dmat = jnp.where((ci[:, None] >= ci[None, :])[None], dmat, 0.0)
    lg_lane = jnp.repeat(log_g, HEAD_DIM).reshape(RET_HEADS // 2, 1, LANES)
    tau = idx.astype(F32)[None, :, None]
    dq = jnp.exp(lg_lane * (tau + 1.0))
    ds = jnp.exp(lg_lane * (blk - 1.0 - tau))
    gc = jnp.exp(lg_lane * float(blk))
    half = HEAD_DIM // 2
    inv = 1.0 / (ROPE_BASE ** (jnp.arange(half, dtype=F32) / half))
    ang = (jnp.arange(t_len) + pos0).astype(F32)[:, None] * inv[None, :]
    cos = jnp.tile(jnp.cos(ang), (1, LANES // half))
    sin = jnp.sin(ang)
    sin = jnp.tile(jnp.concatenate([-sin, sin], axis=1), (1, LANES // HEAD_DIM))
    return cos, sin, dmat, dq, ds, gc


def _dense_out_kernel(x_ref, of_ref, orw_ref, gb_ref, oret_ref, pe_ref, lnw_ref, lnb_ref, bd_ref,
                      wo1_ref, wo2_ref, wo3_ref, n2_ref, wg_ref, wu_ref, wd_ref, wpp_ref, pn_ref, wpg_ref, nf_ref,
                      xo_ref, *, final):
    w_ = RWKV_WIDTH
    hn = _head_norm(orw_ref[...], bd_ref[...], RWKV_GN_EPS)
    orw = ((hn * lnw_ref[...] + lnb_ref[...] + gb_ref[:, w_:2 * w_]) * gb_ref[:, 0:w_]).astype(BF16)
    x = x_ref[...]
    x = x + (jnp.dot(of_ref[...], wo1_ref[...], preferred_element_type=F32)
             + jnp.dot(orw, wo2_ref[...], preferred_element_type=F32)
             + jnp.dot(oret_ref[...], wo3_ref[...], preferred_element_type=F32))
    x = x + 0.5 * _swiglu(_rms(x, n2_ref[...]).astype(BF16), wg_ref, wu_ref, wd_ref)
    e = _rms(jnp.dot(pe_ref[...].astype(BF16), wpp_ref[...], preferred_element_type=F32), pn_ref[...])
    x = x + _sigmoid(jnp.dot(x.astype(BF16), wpg_ref[...], preferred_element_type=F32)) * e
    if final:
        x = _rms(x, nf_ref[...])
    xo_ref[...] = x


def _dense_out(x, of, orw, gb, oret, pe, lnw, lnb, bd, wo1, wo2, wo3, n2, wg, wu, wd, wpp, pn, wpg, nf, final,
               orw_tiles_per_seq, layer):
    n, d = x.shape
    tm = DENSE_TM
    row = lambda a: pl.BlockSpec((tm, a.shape[1]), lambda i: (i, 0))
    pe_spec = pl.BlockSpec((None, tm, pe.shape[2]), lambda i: (layer, i, 0))
    if orw_tiles_per_seq:
        tps = orw_tiles_per_seq
        orw_spec = pl.BlockSpec((tm, RWKV_WIDTH), lambda i: (i % tps, i // tps))
    else:
        orw_spec = row(orw)
    consts = (lnw, lnb, bd, wo1, wo2, wo3, n2, wg, wu, wd, wpp, pn, wpg, nf)
    return pl.pallas_call(
        functools.partial(_dense_out_kernel, final=final),
        grid=(n // tm,),
        in_specs=[row(x), row(of), orw_spec, row(gb), row(oret), pe_spec] + [_resident(c.shape) for c in consts],
        out_specs=row(x),
        out_shape=jax.ShapeDtypeStruct((n, d), F32),
        compiler_params=_cparams(("parallel",)),
        name="dense_out",
    )(x, of, orw, gb, oret, pe, *consts)


def _block_diag_ones(width):
    idx = np.arange(width) // HEAD_DIM
    return jnp.asarray(idx[:, None] == idx[None, :], F32)


def _group(x, pe_all, hist, lw, tables, layer, final):
    (n1, wg1, wu1, wd1, nm, win, wkvt, bfp, mu, w0, a0, k_k, k_a, r_k, wup, aup, gup, lnw, lnb,
     wo1, wo2, wo3, n2, wg2, wu2, wd2, wpp, pn, wpg, nf) = lw
    b, t, d = x.shape
    n = b * t
    bd256, bd128 = _block_diag_ones(RWKV_WIDTH), _block_diag_ones(LANES)
    tiles_per_seq = t // DENSE_TM if t % DENSE_TM == 0 else 0
    x2, q, k, v, fg, lf, zr, zt = _dense_in(x.reshape(n, d), n1, wg1, wu1, wd1, nm, win, wkvt, bfp, tiles_per_seq)
    r3 = lambda a: a.reshape(b, t, a.shape[-1])
    lf3 = r3(lf)
    logf = lf3[:, :, :FOX_HEADS]
    zr3 = r3(zr)
    if hist is None:
        lft = logf.transpose(0, 2, 1)
        ccol, crow = _cumsum(lf3, lft)
        assert tiles_per_seq, "prompt length must be a multiple of the dense tile"
        of = _fox_prompt(r3(q), k, v, r3(fg), ccol, crow)
        fk, fv = (a.reshape(b, FOX_HEADS, HEAD_DIM, t) for a in (k, v))
        prev = jnp.zeros((b, 1, RW_PAD), F32)
        s0 = jnp.zeros((HEAD_DIM, HEAD_DIM, RWKV_HEADS * b), F32)
        r0 = jnp.zeros((b, RET_HEADS // 2, LANES, LANES), F32)
        blk = RET_BLOCK
    else:
        kc, vc, clf, s0, prev, r0 = hist
        past = clf.shape[1]
        lf_all = jnp.concatenate([jnp.pad(clf, ((0, 0), (0, 0), (0, LANES - FOX_HEADS))), lf3], axis=1)
        lft = lf_all[:, :, :FOX_HEADS].transpose(0, 2, 1)
        ccol, crow = _cumsum(lf_all, lft)
        assert not tiles_per_seq
        of = _fox_sample(r3(q), r3(k), r3(v), r3(fg), kc, vc, layer, ccol, crow)
        fk, fv = (a.reshape(b, t, FOX_HEADS, HEAD_DIM) for a in (k, v))
        blk = t
    cos, sin, dmat, dq, ds, gc = tables
    scan_in, post_in = _rwkv_pre(zr3, prev, mu, w0, a0, k_k, k_a, r_k, wup, aup, gup, bd256)
    o_tm, s_fin = _rwkv_scan(scan_in.reshape(t, b, 6 * RWKV_WIDTH), s0)
    if tiles_per_seq:
        orw = o_tm.reshape(t, b * RWKV_WIDTH)
    else:
        orw = o_tm.transpose(1, 0, 2).reshape(n, RWKV_WIDTH)
    oret, r_fin = _retention(r3(zt), cos, sin, dmat, dq, ds, gc, r0, bd128, blk)
    xo = _dense_out(x2, of.reshape(n, FOX_WIDTH), orw, post_in.reshape(n, 2 * RWKV_WIDTH), oret.reshape(n, RET_WIDTH),
                    pe_all.reshape(pe_all.shape[0], n, pe_all.shape[-1]), lnw, lnb, bd256, wo1, wo2, wo3, n2, wg2, wu2,
                    wd2, wpp, pn, wpg, nf, final, tiles_per_seq, layer)
    s_new = s_fin.reshape(HEAD_DIM, HEAD_DIM, RWKV_HEADS, b).transpose(3, 2, 1, 0)
    r_new = jnp.stack([r_fin[:, hh // 2, (hh % 2) * HEAD_DIM:(hh % 2 + 1) * HEAD_DIM,
                             (hh % 2) * HEAD_DIM:(hh % 2 + 1) * HEAD_DIM] for hh in range(RET_HEADS)], axis=1)
    state = (fk, fv, logf, s_new, zr3[:, t - 1:t, :], r_new)
    return xo.reshape(b, t, d), state


def kernel(x_prompt, x_sample, cache_fox_k, cache_fox_v, cache_fox_logf, state_rwkv, state_rwkv_shift, state_ret,
           p_prompt, p_sample, norm_ffn1, w_ffn1_gate, w_ffn1_up, w_ffn1_down, norm_mix, w_in, fox_forget_bias,
           rwkv_mu, rwkv_w0, rwkv_w_up, rwkv_a0, rwkv_a_up, rwkv_g_up, rwkv_k_k, rwkv_k_a, rwkv_r_k, rwkv_ln_w,
           rwkv_ln_b, w_out, norm_ffn2, w_ffn2_gate, w_ffn2_up, w_ffn2_down, w_ple_proj, ple_norm, w_ple_gate,
           norm_final):
    depth, d, _ = w_in.shape
    bp, tp, _ = x_prompt.shape
    bs, ts, _ = x_sample.shape
    past = cache_fox_k.shape[2]
    assert RWKV_HEADS * bp == LANES and RWKV_HEADS * bs == LANES, "rwkv scan puts (head, batch) on the 128 lanes"
    assert tp % RET_BLOCK == 0 and tp % FOX_BQ == 0 and (bp * tp) % DENSE_TM == 0 and (bs * ts) % DENSE_TM == 0

    fw, rw = FOX_WIDTH, RWKV_WIDTH
    o_f = 3 * fw
    o_g = o_f + FOX_HEADS
    o_rw = o_g + fw
    o_ret = o_rw + 3 * rw + 3 * RWKV_RANK
    zpad = lambda w: jnp.zeros((depth, d, w), F32)
    r_ = RWKV_RANK
    rw_cols = 3 * rw + 3 * r_

    def regroup(a):
        parts = [a[..., 0:rw], a[..., rw + r_:2 * rw + r_], a[..., 2 * rw + r_:3 * rw + r_], a[..., rw:rw + r_],
                 a[..., 3 * rw + r_:rw_cols], jnp.zeros(a.shape[:-1] + (RW_PAD - rw_cols,), a.dtype)]
        return jnp.concatenate(parts, axis=-1)

    def ungroup(a):
        parts = [a[..., 0:rw], a[..., 3 * rw:3 * rw + r_], a[..., rw:3 * rw], a[..., 3 * rw + r_:rw_cols]]
        return jnp.concatenate(parts, axis=-1)

    win_all = jnp.concatenate([w_in[:, :, 0:o_f], w_in[:, :, o_g:o_rw], w_in[:, :, o_f:o_g], zpad(LANES - FOX_HEADS),
                               regroup(w_in[:, :, o_rw:o_ret]), w_in[:, :, o_ret:]], axis=2).astype(BF16)
    wkvt = win_all[:, :, C_K:C_G].transpose(0, 2, 1)
    bfp = jnp.pad(fox_forget_bias, ((0, 0), (0, LANES - FOX_HEADS)))[:, None, :]
    mu_p = regroup(rwkv_mu)[:, None, :]
    lowrank = lambda w, slot: jnp.pad(w, ((0, 0), (slot * RWKV_RANK, LANES - (slot + 1) * RWKV_RANK), (0, 0))).astype(BF16)
    wup, aup, gup = lowrank(rwkv_w_up, 0), lowrank(rwkv_a_up, 1), lowrank(rwkv_g_up, 2)
    vec = lambda a: a.reshape(depth, 1, -1)
    bf = lambda a: a.astype(BF16)
    wg1, wu1, wd1 = bf(w_ffn1_gate), bf(w_ffn1_up), bf(w_ffn1_down)
    wg2, wu2, wd2 = bf(w_ffn2_gate), bf(w_ffn2_up), bf(w_ffn2_down)
    wo = bf(w_out)
    wpp, wpg = bf(w_ple_proj), bf(w_ple_gate)
    nf = norm_final.reshape(1, -1)

    def layer_weights(l):
        return (vec(norm_ffn1)[l], wg1[l], wu1[l], wd1[l], vec(norm_mix)[l], win_all[l], wkvt[l], bfp[l], mu_p[l],
                vec(rwkv_w0)[l], vec(rwkv_a0)[l], vec(rwkv_k_k)[l], vec(rwkv_k_a)[l], vec(rwkv_r_k)[l],
                wup[l], aup[l], gup[l], vec(rwkv_ln_w)[l], vec(rwkv_ln_b)[l],
                wo[l, 0:fw], wo[l, fw:fw + rw], wo[l, fw + rw:], vec(norm_ffn2)[l], wg2[l], wu2[l], wd2[l],
                wpp[l], vec(ple_norm)[l], wpg[l], nf)

    kc = cache_fox_k.transpose(0, 1, 3, 4, 2)
    vc = cache_fox_v.transpose(0, 1, 3, 4, 2)
    s0_all = state_rwkv.transpose(0, 4, 3, 2, 1).reshape(depth, HEAD_DIM, HEAD_DIM, RWKV_HEADS * bs)
    prev_all = regroup(state_rwkv_shift)
    eye2 = jnp.eye(2, dtype=F32)
    r0_all = jnp.einsum('lbpqde,qr->lbpqdre', state_ret.reshape(depth, bs, RET_HEADS // 2, 2, HEAD_DIM, HEAD_DIM),
                        eye2).reshape(depth, bs, RET_HEADS // 2, LANES, LANES)

    tab_p = _retention_tables(RET_BLOCK, RET_CHUNK, tp, 0)
    tab_s = _retention_tables(ts, ts, ts, past)

    xp, xs = x_prompt, x_sample
    p_states, s_states = [], []
    for l in range(depth):
        lw = layer_weights(l)
        final = l == depth - 1
        xp, sp = _group(xp, p_prompt, None, lw, tab_p, l, final)
        hist = (kc, vc, cache_fox_logf[l], s0_all[l], prev_all[l], r0_all[l])
        xs, ss = _group(xs, p_sample, hist, lw, tab_s, l, final)
        p_states.append(sp)
        s_states.append(ss)

    def stacked(states, kv_transposed):
        cols = [jnp.stack([s[i] for s in states]) for i in range(6)]
        if kv_transposed:
            cols[0], cols[1] = cols[0].transpose(0, 1, 4, 2, 3), cols[1].transpose(0, 1, 4, 2, 3)
        cols[4] = ungroup(cols[4])
        return cols

    return (xp, xs, *stacked(p_states, True), *stacked(s_states, False))
```

```python
import functools

import jax
import jax.numpy as jnp
import numpy as np
from jax import lax
from jax.experimental import pallas as pl
from jax.experimental.pallas import tpu as pltpu

F32 = jnp.float32
BF16 = jnp.bfloat16

HEAD_DIM = 64
FOX_HEADS = 8
RWKV_HEADS = 4
RET_HEADS = 4
FOX_WIDTH = FOX_HEADS * HEAD_DIM
RWKV_WIDTH = RWKV_HEADS * HEAD_DIM
RET_WIDTH = RET_HEADS * HEAD_DIM
RWKV_RANK = 32
RET_CHUNK = 64
ROPE_BASE = 10000.0
NORM_EPS = 1e-6
RWKV_GN_EPS = 64e-5
RET_GN_EPS = 1e-6

LANES = 128
VMEM_LIMIT_BYTES = 58 * 1024 * 1024
DENSE_TM = 256
DENSE_OUT_TM = 512
FFN_CHUNK = 1408
FOX_BQ = 256
CUM_BLK = 256
RWKV_TT = 256
SCAN_TC = 8
SCAN_K_UNROLL = 8
SCAN_PAIRS = 3
F_R, F_W, F_K, F_V, F_KK, F_KKA = range(6)
RET_BLOCK = 256

C_Q, C_K, C_V, C_G, C_F, C_RW, C_RET, C_END = 0, 512, 1024, 1536, 2048, 2176, 3072, 4096
RW_PAD = C_RET - C_RW


def _cparams(sem):
    return pltpu.CompilerParams(dimension_semantics=sem, vmem_limit_bytes=VMEM_LIMIT_BYTES)


def _resident(shape):
    nd = len(shape)
    return pl.BlockSpec(shape, lambda *_: (0,) * nd, pipeline_mode=pl.Buffered(1))


def _rms(x, g):
    return x * lax.rsqrt(jnp.mean(x * x, axis=-1, keepdims=True) + NORM_EPS) * g


def _sigmoid(x):
    return 1.0 / (1.0 + jnp.exp(-x))


def _log1pexp_neg_abs(x):
    return jnp.log(1.0 + jnp.exp(-jnp.abs(x)))


def _split2(x):
    hi = x.astype(BF16).astype(F32)
    lo = (x - hi).astype(BF16).astype(F32)
    return hi, lo


def _seg_sum(x, bd):
    hi, lo = _split2(x)
    return jnp.dot(hi, bd, preferred_element_type=F32) + jnp.dot(lo, bd, preferred_element_type=F32)


def _head_norm(o, bd, eps):
    mu = _seg_sum(o, bd) * (1.0 / HEAD_DIM)
    d = o - mu
    var = _seg_sum(d * d, bd) * (1.0 / HEAD_DIM)
    return d * lax.rsqrt(var + eps)


def _swiglu(h, wg_ref, wu_ref, wd_ref):
    d_ff = wg_ref.shape[1]
    acc = None
    for c0 in range(0, d_ff, FFN_CHUNK):
        g = jnp.dot(h, wg_ref[:, c0:c0 + FFN_CHUNK], preferred_element_type=F32)
        u = jnp.dot(h, wu_ref[:, c0:c0 + FFN_CHUNK], preferred_element_type=F32)
        a = (g * _sigmoid(g) * u).astype(BF16)
        d = jnp.dot(a, wd_ref[c0:c0 + FFN_CHUNK, :], preferred_element_type=F32)
        acc = d if acc is None else acc + d
    return acc


def _dense_in_kernel(x_ref, n1_ref, wg_ref, wu_ref, wd_ref, nm_ref, win_ref, wkvt_ref, bf_ref,
                     xo_ref, q_ref, k_ref, v_ref, g_ref, lf_ref, zr_ref, zt_ref, *, kv_transposed):
    x = x_ref[...]
    x = x + 0.5 * _swiglu(_rms(x, n1_ref[...]).astype(BF16), wg_ref, wu_ref, wd_ref)
    xo_ref[...] = x
    h = _rms(x, nm_ref[...]).astype(BF16)
    if kv_transposed:
        zq = jnp.dot(h, win_ref[:, C_Q:C_K], preferred_element_type=F32)
        kvt = lax.dot_general(wkvt_ref[...], h, (((1,), (1,)), ((), ())), preferred_element_type=F32)
        k_ref[0] = kvt[:FOX_WIDTH]
        v_ref[0] = kvt[FOX_WIDTH:]
        z = jnp.dot(h, win_ref[:, C_G:C_END], preferred_element_type=F32)
        off = C_G
    else:
        z = jnp.dot(h, win_ref[...], preferred_element_type=F32)
        zq = z[:, C_Q:C_K]
        k_ref[...] = z[:, C_K:C_V]
        v_ref[...] = z[:, C_V:C_G]
        off = 0
    q_ref[...] = (zq * (HEAD_DIM ** -0.5)).astype(BF16)
    g_ref[...] = z[:, C_G - off:C_F - off]
    f = z[:, C_F - off:C_RW - off] + bf_ref[...]
    lf_ref[...] = jnp.minimum(f, 0.0) - _log1pexp_neg_abs(f)
    zr_ref[...] = z[:, C_RW - off:C_RET - off]
    zt_ref[...] = z[:, C_RET - off:C_END - off]


def _dense_in(x, n1, wg, wu, wd, nm, win, wkvt, bfp, tiles_per_seq):
    n, d = x.shape
    tm = DENSE_TM
    row = lambda w: pl.BlockSpec((tm, w), lambda i: (i, 0))
    outs = [(d, F32), (FOX_WIDTH, BF16), (FOX_WIDTH, F32), (FOX_WIDTH, F32), (FOX_WIDTH, F32),
            (LANES, F32), (RW_PAD, F32), (C_END - C_RET, F32)]
    out_specs = [row(w) for w, _ in outs]
    out_shape = [jax.ShapeDtypeStruct((n, w), dt) for w, dt in outs]
    if tiles_per_seq:
        tps = tiles_per_seq
        for j in (2, 3):
            out_specs[j] = pl.BlockSpec((1, FOX_WIDTH, tm), lambda i: (i // tps, 0, i % tps))
            out_shape[j] = jax.ShapeDtypeStruct((n // (tps * tm), FOX_WIDTH, tps * tm), F32)
    return pl.pallas_call(
        functools.partial(_dense_in_kernel, kv_transposed=bool(tiles_per_seq)),
        grid=(n // tm,),
        in_specs=[row(d), _resident(n1.shape), _resident(wg.shape), _resident(wu.shape), _resident(wd.shape),
                  _resident(nm.shape), _resident(win.shape), _resident(wkvt.shape), _resident(bfp.shape)],
        out_specs=out_specs,
        out_shape=out_shape,
        compiler_params=_cparams(("parallel",)),
        name="dense_in",
    )(x, n1, wg, wu, wd, nm, win, wkvt, bfp)


def _cumsum_kernel(lf_ref, lft_ref, ccol_ref, crow_ref, *, t_len):
    nh = lft_ref.shape[1]
    r = lax.broadcasted_iota(jnp.int32, (CUM_BLK, CUM_BLK), 0)
    c = lax.broadcasted_iota(jnp.int32, (CUM_BLK, CUM_BLK), 1)
    tri_l = (r >= c).astype(F32)
    tri_u = (r <= c).astype(F32)
    carry_c = jnp.zeros((1, LANES), F32)
    carry_r = jnp.zeros((nh, 1), F32)
    for s in range(0, t_len, CUM_BLK):
        n = min(CUM_BLK, t_len - s)
        x = lf_ref[0, s:s + n, :]
        hi = x.astype(BF16).astype(F32)
        r1 = x - hi
        mid = r1.astype(BF16).astype(F32)
        lo = r1 - mid
        tl = tri_l[:n, :n]
        cc = (jnp.dot(tl, hi, preferred_element_type=F32) + jnp.dot(tl, mid, preferred_element_type=F32)
              + jnp.dot(tl, lo, preferred_element_type=F32)) + carry_c
        ccol_ref[0, s:s + n, :] = cc
        carry_c = cc[n - 1:n, :]
        y = lft_ref[0, :, s:s + n]
        hi = y.astype(BF16).astype(F32)
        r1 = y - hi
        mid = r1.astype(BF16).astype(F32)
        lo = r1 - mid
        tu = tri_u[:n, :n]
        cr = (jnp.dot(hi, tu, preferred_element_type=F32) + jnp.dot(mid, tu, preferred_element_type=F32)
              + jnp.dot(lo, tu, preferred_element_type=F32)) + carry_r
        crow_ref[0, :, s:s + n] = cr
        carry_r = cr[:, n - 1:n]


def _cumsum(lf3, lft):
    b, t, _ = lf3.shape
    nh = lft.shape[1]
    return pl.pallas_call(
        functools.partial(_cumsum_kernel, t_len=t),
        grid=(b,),
        in_specs=[pl.BlockSpec((1, t, LANES), lambda i: (i, 0, 0)), pl.BlockSpec((1, nh, t), lambda i: (i, 0, 0))],
        out_specs=[pl.BlockSpec((1, t, LANES), lambda i: (i, 0, 0)), pl.BlockSpec((1, nh, t), lambda i: (i, 0, 0))],
        out_shape=[jax.ShapeDtypeStruct((b, t, LANES), F32), jax.ShapeDtypeStruct((b, nh, t), F32)],
        compiler_params=_cparams(("parallel",)),
        name="logf_cumsum",
    )(lf3, lft)


def _qk(q, k):
    return lax.dot_general(q, k, (((1,), (1,)), ((), ())), preferred_element_type=F32)


def _softmax_pv(s_list, v_list, v_transposed):
    m = None
    for s in s_list:
        mi = jnp.max(s, axis=1, keepdims=True)
        m = mi if m is None else jnp.maximum(m, mi)
    l = None
    o = None
    for s, v, vt in zip(s_list, v_list, v_transposed):
        p = jnp.exp(s - m)
        li = jnp.sum(p, axis=1, keepdims=True)
        oi = _qk(p.astype(BF16), v) if vt else jnp.dot(p.astype(BF16), v, preferred_element_type=F32)
        l = li if l is None else l + li
        o = oi if o is None else o + oi
    return o / l


def _fox_prompt_kernel(q_ref, k_ref, v_ref, g_ref, ccol_ref, crow_ref, o_ref, *, t_len):
    pair = pl.program_id(1)
    lane = lax.broadcasted_iota(jnp.int32, (1, LANES), 1)
    low = lane < HEAD_DIM
    kb = k_ref[0].astype(BF16)
    vb = v_ref[0].astype(BF16)
    rr = lax.broadcasted_iota(jnp.int32, (FOX_BQ, FOX_BQ), 0)
    cc = lax.broadcasted_iota(jnp.int32, (FOX_BQ, FOX_BQ), 1)
    causal = rr >= cc
    for s0 in range(0, t_len, FOX_BQ):
        e = s0 + FOX_BQ
        q = q_ref[0, s0:e, :]
        ccol = ccol_ref[0, s0:e, :]
        res = []
        for hh in range(2):
            h = 2 * pair + hh
            msk = low if hh == 0 else jnp.logical_not(low)
            qm = jnp.where(msk, q, jnp.zeros_like(q))
            cq = jnp.sum(jnp.where(lane == h, ccol, 0.0), axis=1, keepdims=True)
            ck = crow_ref[0, pl.ds(h, 1), :]
            sd = jnp.dot(qm, kb[:, s0:e], preferred_element_type=F32) + cq - ck[:, s0:e]
            sd = jnp.where(causal, sd, -jnp.inf)
            if s0 > 0:
                sp = jnp.dot(qm, kb[:, :s0], preferred_element_type=F32) + cq - ck[:, :s0]
                res.append(_softmax_pv([sp, sd], [vb[:, :s0], vb[:, s0:e]], [True, True]))
            else:
                res.append(_softmax_pv([sd], [vb[:, s0:e]], [True]))
        o = jnp.where(low, res[0], res[1]) * _sigmoid(g_ref[0, s0:e, :])
        o_ref[0, s0:e, :] = o.astype(o_ref.dtype)


def _fox_prompt(q3, kt, vt, g3, ccol, crow):
    b, t, w = q3.shape
    npair = w // LANES
    nh = crow.shape[1]
    blk = lambda: pl.BlockSpec((1, t, LANES), lambda i, p: (i, 0, p))
    blk_t = lambda: pl.BlockSpec((1, LANES, t), lambda i, p: (i, p, 0))
    return pl.pallas_call(
        functools.partial(_fox_prompt_kernel, t_len=t),
        grid=(b, npair),
        in_specs=[blk(), blk_t(), blk_t(), blk(),
                  pl.BlockSpec((1, t, LANES), lambda i, p: (i, 0, 0)),
                  pl.BlockSpec((1, nh, t), lambda i, p: (i, 0, 0))],
        out_specs=blk(),
        out_shape=jax.ShapeDtypeStruct((b, t, w), BF16),
        compiler_params=_cparams(("parallel", "parallel")),
        name="fox_prompt",
    )(q3, kt, vt, g3, ccol, crow)


def _fox_sample_kernel(q_ref, kn_ref, vn_ref, g_ref, kc_ref, vc_ref, ccol_ref, crow_ref, o_ref, *, past, n_new):
    pair = pl.program_id(1)
    lane = lax.broadcasted_iota(jnp.int32, (1, LANES), 1)
    low = lane < HEAD_DIM
    kc = kc_ref[0, 0].reshape(LANES, past).astype(BF16)
    vc = vc_ref[0, 0].reshape(LANES, past).astype(BF16)
    kn = kn_ref[0].astype(BF16)
    vn = vn_ref[0].astype(BF16)
    q = q_ref[0]
    ccol = ccol_ref[0, past:past + n_new, :]
    rr = lax.broadcasted_iota(jnp.int32, (n_new, n_new), 0)
    cc = lax.broadcasted_iota(jnp.int32, (n_new, n_new), 1)
    causal = rr >= cc
    res = []
    for hh in range(2):
        h = 2 * pair + hh
        msk = low if hh == 0 else jnp.logical_not(low)
        qm = jnp.where(msk, q, jnp.zeros_like(q))
        cq = jnp.sum(jnp.where(lane == h, ccol, 0.0), axis=1, keepdims=True)
        ck = crow_ref[0, pl.ds(h, 1), :]
        sc = jnp.dot(qm, kc, preferred_element_type=F32) + cq - ck[:, :past]
        sn = _qk(qm, kn) + cq - ck[:, past:past + n_new]
        sn = jnp.where(causal, sn, -jnp.inf)
        res.append(_softmax_pv([sc, sn], [vc, vn], [True, False]))
    o = jnp.where(low, res[0], res[1]) * _sigmoid(g_ref[0])
    o_ref[0] = o.astype(o_ref.dtype)


def _fox_sample(q3, k3, v3, g3, kc, vc, layer, ccol, crow):
    b, n_new, w = q3.shape
    past = kc.shape[4]
    npair = w // LANES
    nh = crow.shape[1]
    tot = past + n_new
    blk = lambda: pl.BlockSpec((1, n_new, LANES), lambda i, p: (i, 0, p))
    cache = lambda: pl.BlockSpec((1, 1, 2, HEAD_DIM, past), lambda i, p: (layer, i, p, 0, 0))
    return pl.pallas_call(
        functools.partial(_fox_sample_kernel, past=past, n_new=n_new),
        grid=(b, npair),
        in_specs=[blk(), blk(), blk(), blk(), cache(), cache(),
                  pl.BlockSpec((1, tot, LANES), lambda i, p: (i, 0, 0)),
                  pl.BlockSpec((1, nh, tot), lambda i, p: (i, 0, 0))],
        out_specs=blk(),
        out_shape=jax.ShapeDtypeStruct((b, n_new, w), BF16),
        compiler_params=_cparams(("parallel", "parallel")),
        name="fox_sample",
    )(q3, k3, v3, g3, kc, vc, ccol, crow)


def _rwkv_pre_kernel(z_ref, zp_ref, prev_ref, mu_ref, w0_ref, a0_ref, kk_ref, ka_ref, rk_ref,
                     wup_ref, aup_ref, gup_ref, bd_ref, scan_ref, post_ref):
    i = pl.program_id(1)
    z = z_ref[0]
    tt = z.shape[0]
    p_in = jnp.where(i == 0, prev_ref[0], zp_ref[0, 7:8, :])
    row = lax.broadcasted_iota(jnp.int32, (tt, 1), 0)
    shifted = jnp.where(row == 0, p_in, pltpu.roll(z, 1, axis=0))
    zs = z + mu_ref[...] * (shifted - z)
    w_ = RWKV_WIDTH
    r, k, v, low = zs[:, 0:w_], zs[:, w_:2 * w_], zs[:, 2 * w_:3 * w_], zs[:, 3 * w_:]
    bd = bd_ref[...]
    y = w0_ref[...] + jnp.dot(jnp.tanh(low).astype(BF16), wup_ref[...], preferred_element_type=F32)
    sp = jnp.maximum(-y, 0.0) + _log1pexp_neg_abs(y)
    decay = jnp.exp(-jnp.exp(-sp - 0.5))
    a = _sigmoid(a0_ref[...] + jnp.dot(low.astype(BF16), aup_ref[...], preferred_element_type=F32))
    g = jnp.dot(_sigmoid(low).astype(BF16), gup_ref[...], preferred_element_type=F32)
    kk = k * kk_ref[...]
    kk = kk * lax.rsqrt(jnp.maximum(_seg_sum(kk * kk, bd), 1e-24))
    k2 = k * (1.0 + (a - 1.0) * ka_ref[...])
    bonus = _seg_sum(r * k2 * rk_ref[...], bd) * v
    fields = {F_R: r, F_W: decay, F_K: k2, F_V: v, F_KK: kk, F_KKA: kk * a}
    per_head = 2 * SCAN_PAIRS * HEAD_DIM
    for h in range(RWKV_HEADS):
        for f, val in fields.items():
            c0 = h * per_head + f * HEAD_DIM
            scan_ref[:, c0:c0 + HEAD_DIM] = val[:, h * HEAD_DIM:(h + 1) * HEAD_DIM]
    post_ref[0, :, 0:w_] = g
    post_ref[0, :, w_:2 * w_] = bonus


def _rwkv_pre(zr3, prev, mu, w0, a0, k_k, k_a, r_k, wup, aup, gup, bd):
    b, t, w = zr3.shape
    tt = min(RWKV_TT, t)
    full = lambda a: pl.BlockSpec(a.shape, lambda i, j: (0,) * a.ndim)
    return pl.pallas_call(
        _rwkv_pre_kernel,
        grid=(b, t // tt),
        in_specs=[pl.BlockSpec((1, tt, w), lambda i, j: (i, j, 0)),
                  pl.BlockSpec((1, 8, w), lambda i, j: (i, jnp.maximum(j * (tt // 8) - 1, 0), 0)),
                  pl.BlockSpec((1, 1, w), lambda i, j: (i, 0, 0)),
                  full(mu), full(w0), full(a0), full(k_k), full(k_a), full(r_k), full(wup), full(aup), full(gup),
                  full(bd)],
        out_specs=[pl.BlockSpec((tt, 6 * RWKV_WIDTH), lambda i, j: (j, i)),
                   pl.BlockSpec((1, tt, 2 * RWKV_WIDTH), lambda i, j: (i, j, 0))],
        out_shape=[jax.ShapeDtypeStruct((t, b * 6 * RWKV_WIDTH), F32), jax.ShapeDtypeStruct((b, t, 2 * RWKV_WIDTH), F32)],
        compiler_params=_cparams(("parallel", "parallel")),
        name="rwkv_pre",
    )(zr3, zr3, prev, mu, w0, a0, k_k, k_a, r_k, wup, aup, gup, bd)


def _rwkv_scan_kernel(*refs):
    x_ref, s0_ref, _, o_ref, s_ref, x_scr, o_scr = refs
    i = pl.program_id(0)
    tc, nb = x_ref.shape[0], x_ref.shape[1]
    ng = HEAD_DIM // 8

    @pl.when(i == 0)
    def _():
        s_ref[...] = s0_ref[...]

    for t in range(tc):
        for fp in range(SCAN_PAIRS):
            cols = [(h * SCAN_PAIRS + fp) * LANES for h in range(RWKV_HEADS)]
            y = jnp.concatenate([x_ref[t, :, c:c + LANES] for c in cols], axis=0)
            yt = y.T
            x_scr[t, 2 * fp] = yt[0:HEAD_DIM]
            x_scr[t, 2 * fp + 1] = yt[HEAD_DIM:2 * HEAD_DIM]

    def step(t, carry):
        def dot_kk(k, acc):
            kk = x_scr[t, F_KK, pl.ds(k, 1), :]
            return tuple(acc[g] + s_ref[k, g * 8:(g + 1) * 8, :] * kk for g in range(ng))

        zero = tuple(jnp.zeros((8, LANES), F32) for _ in range(ng))
        skk = lax.fori_loop(0, HEAD_DIM, dot_kk, zero, unroll=SCAN_K_UNROLL)
        vv = tuple(x_scr[t, F_V, g * 8:(g + 1) * 8, :] for g in range(ng))

        def update(k, acc):
            w = x_scr[t, F_W, pl.ds(k, 1), :]
            kka = x_scr[t, F_KKA, pl.ds(k, 1), :]
            kx = x_scr[t, F_K, pl.ds(k, 1), :]
            r = x_scr[t, F_R, pl.ds(k, 1), :]
            out = []
            for g in range(ng):
                sn = s_ref[k, g * 8:(g + 1) * 8, :] * w - skk[g] * kka + vv[g] * kx
                s_ref[k, g * 8:(g + 1) * 8, :] = sn
                out.append(acc[g] + sn * r)
            return tuple(out)

        o = lax.fori_loop(0, HEAD_DIM, update, zero, unroll=SCAN_K_UNROLL)
        for g in range(ng):
            o_scr[t, g * 8:(g + 1) * 8, :] = o[g]
        return carry

    lax.fori_loop(0, tc, step, 0)

    low = lax.broadcasted_iota(jnp.int32, (1, LANES), 1) < HEAD_DIM
    for t in range(0, tc, 2):
        z = jnp.concatenate([o_scr[t], o_scr[t + 1]], axis=0).T
        zr = pltpu.roll(z, HEAD_DIM, axis=1)
        for p in range(RWKV_HEADS // 2):
            ev = slice(2 * p * nb, (2 * p + 1) * nb)
            od = slice((2 * p + 1) * nb, (2 * p + 2) * nb)
            o_ref[t, :, p * LANES:(p + 1) * LANES] = jnp.where(low, z[ev], zr[od])
            o_ref[t + 1, :, p * LANES:(p + 1) * LANES] = jnp.where(low, zr[ev], z[od])


def _rwkv_scan(scan_in, s0, after):
    t, b, w = scan_in.shape
    tc = min(SCAN_TC, t)
    state = pl.BlockSpec((HEAD_DIM, HEAD_DIM, LANES), lambda i: (0, 0, 0))
    return pl.pallas_call(
        _rwkv_scan_kernel,
        grid=(t // tc,),
        in_specs=[pl.BlockSpec((tc, b, w), lambda i: (i, 0, 0)), state,
                  pl.BlockSpec((1, 16, LANES), lambda i: (0, 0, 0))],
        out_specs=[pl.BlockSpec((tc, b, RWKV_WIDTH), lambda i: (i, 0, 0)), state],
        out_shape=[jax.ShapeDtypeStruct((t, b, RWKV_WIDTH), F32),
                   jax.ShapeDtypeStruct((HEAD_DIM, HEAD_DIM, LANES), F32)],
        scratch_shapes=[pltpu.VMEM((tc, 2 * SCAN_PAIRS, HEAD_DIM, LANES), F32), pltpu.VMEM((tc, HEAD_DIM, LANES), F32)],
        compiler_params=_cparams(("arbitrary",)),
        name="rwkv_scan",
    )(scan_in, s0, after)


def _retention_kernel(z_ref, cos_ref, sin_ref, d_ref, dq_ref, ds_ref, gc_ref, r0_ref, bd_ref,
                      o_ref, r_scr):
    i = pl.program_id(1)

    @pl.when(i == 0)
    def _():
        r_scr[...] = r0_ref[...]

    cos = cos_ref[...]
    sin = sin_ref[...]
    lane = lax.broadcasted_iota(jnp.int32, (1, LANES), 1)
    low = lane < HEAD_DIM
    first_half = (lane % HEAD_DIM) < (HEAD_DIM // 2)
    rr = lax.broadcasted_iota(jnp.int32, (LANES, LANES), 0)
    cc = lax.broadcasted_iota(jnp.int32, (LANES, LANES), 1)
    same_head = (rr < HEAD_DIM) == (cc < HEAD_DIM)
    bd = bd_ref[...]
    w_ = RET_WIDTH

    def rot(x):
        swapped = jnp.where(first_half, pltpu.roll(x, LANES - HEAD_DIM // 2, axis=1), pltpu.roll(x, HEAD_DIM // 2, axis=1))
        return x * cos + swapped * sin

    for p in range(RET_HEADS // 2):
        c0 = p * LANES
        qr = rot(z_ref[0, :, c0:c0 + LANES])
        kr = rot(z_ref[0, :, w_ + c0:w_ + c0 + LANES]) * (HEAD_DIM ** -0.5)
        vb = z_ref[0, :, 2 * w_ + c0:2 * w_ + c0 + LANES].astype(BF16)
        gate = z_ref[0, :, 3 * w_ + c0:3 * w_ + c0 + LANES]
        qb = qr.astype(BF16)
        kb = kr.astype(BF16)
        r_old = r_scr[0, p]
        o = jnp.dot((qr * dq_ref[p]).astype(BF16), r_old.astype(BF16), preferred_element_type=F32)
        for hh in range(2):
            msk = low if hh == 0 else jnp.logical_not(low)
            qm = jnp.where(msk, qb, jnp.zeros_like(qb))
            pmat = (_qk(qm, kb) * d_ref[2 * p + hh]).astype(BF16)
            o = o + jnp.where(msk, jnp.dot(pmat, vb, preferred_element_type=F32), 0.0)
        kd = (kr * ds_ref[p]).astype(BF16)
        kv = lax.dot_general(kd, vb, (((0,), (0,)), ((), ())), preferred_element_type=F32)
        r_scr[0, p] = r_old * gc_ref[p] + jnp.where(same_head, kv, 0.0)
        out = _head_norm(o, bd, RET_GN_EPS) * (gate * _sigmoid(gate))
        o_ref[0, :, c0:c0 + LANES] = out.astype(o_ref.dtype)


def _retention(zt3, cos, sin, dmat, dq, ds, gc, r0, bd, blk):
    b, t, w = zt3.shape
    npair = RET_HEADS // 2
    full = lambda a: pl.BlockSpec(a.shape, lambda i, j: (0,) * a.ndim)
    return pl.pallas_call(
        _retention_kernel,
        grid=(b, t // blk),
        in_specs=[pl.BlockSpec((1, blk, w), lambda i, j: (i, j, 0)),
                  pl.BlockSpec((blk, LANES), lambda i, j: (j, 0)),
                  pl.BlockSpec((blk, LANES), lambda i, j: (j, 0)),
                  full(dmat), full(dq), full(ds), full(gc),
                  pl.BlockSpec((1, npair, LANES, LANES), lambda i, j: (i, 0, 0, 0)),
                  full(bd)],
        out_specs=[pl.BlockSpec((1, blk, RET_WIDTH), lambda i, j: (i, j, 0)),
                   pl.BlockSpec((1, npair, LANES, LANES), lambda i, j: (i, 0, 0, 0))],
        out_shape=[jax.ShapeDtypeStruct((b, t, RET_WIDTH), BF16),
                   jax.ShapeDtypeStruct((b, npair, LANES, LANES), F32)],
        compiler_params=_cparams(("parallel", "arbitrary")),
        name="retention",
    )(zt3, cos, sin, dmat, dq, ds, gc, r0, bd)


def _retention_tables(blk, chunk, t_len, pos0):
    log_g = jnp.log(1.0 - jnp.power(2.0, -5.0 - jnp.arange(RET_HEADS, dtype=F32)))
    idx = jnp.arange(blk)
    ci = idx // chunk
    diff = (idx[:, None] - idx[None, :]).astype(F32)
    expo = jnp.where(ci[:, None] == ci[None, :], jnp.abs(diff), diff)
    dmat = jnp.exp(log_g[:, None, None] * expo[None])
    dmat = jnp.where((ci[:, None] >= ci[None, :])[None], dmat, 0.0)
    lg_lane = jnp.repeat(log_g, HEAD_DIM).reshape(RET_HEADS // 2, 1, LANES)
    tau = idx.astype(F32)[None, :, None]
    dq = jnp.exp(lg_lane * (tau + 1.0))
    ds = jnp.exp(lg_lane * (blk - 1.0 - tau))
    gc = jnp.exp(lg_lane * float(blk))
    half = HEAD_DIM // 2
    inv = 1.0 / (ROPE_BASE ** (jnp.arange(half, dtype=F32) / half))
    ang = (jnp.arange(t_len) + pos0).astype(F32)[:, None] * inv[None, :]
    cos = jnp.tile(jnp.cos(ang), (1, LANES // half))
    sin = jnp.sin(ang)
    sin = jnp.tile(jnp.concatenate([-sin, sin], axis=1), (1, LANES // HEAD_DIM))
    return cos, sin, dmat, dq, ds, gc


def _dense_out_kernel(x_ref, of_ref, orw_ref, gb_ref, oret_ref, pe_ref, lnw_ref, lnb_ref, bd_ref,
                      wo1_ref, wo2_ref, wo3_ref, n2_ref, wg_ref, wu_ref, wd_ref, wpp_ref, pn_ref, wpg_ref, nf_ref,
                      xo_ref, *, final):
    w_ = RWKV_WIDTH
    hn = _head_norm(orw_ref[...], bd_ref[...], RWKV_GN_EPS)
    orw = ((hn * lnw_ref[...] + lnb_ref[...] + gb_ref[:, w_:2 * w_]) * gb_ref[:, 0:w_]).astype(BF16)
    x = x_ref[...]
    x = x + (jnp.dot(of_ref[...], wo1_ref[...], preferred_element_type=F32)
             + jnp.dot(orw, wo2_ref[...], preferred_element_type=F32)
             + jnp.dot(oret_ref[...], wo3_ref[...], preferred_element_type=F32))
    x = x + 0.5 * _swiglu(_rms(x, n2_ref[...]).astype(BF16), wg_ref, wu_ref, wd_ref)
    e = _rms(jnp.dot(pe_ref[...].astype(BF16), wpp_ref[...], preferred_element_type=F32), pn_ref[...])
    x = x + _sigmoid(jnp.dot(x.astype(BF16), wpg_ref[...], preferred_element_type=F32)) * e
    if final:
        x = _rms(x, nf_ref[...])
    xo_ref[...] = x


def _dense_out(x, of, orw, gb, oret, pe, lnw, lnb, bd, wo1, wo2, wo3, n2, wg, wu, wd, wpp, pn, wpg, nf, final,
               orw_tiles_per_seq, layer):
    n, d = x.shape
    tm = DENSE_OUT_TM
    row = lambda a: pl.BlockSpec((tm, a.shape[1]), lambda i: (i, 0))
    pe_spec = pl.BlockSpec((None, tm, pe.shape[2]), lambda i: (layer, i, 0))
    if orw_tiles_per_seq:
        tps = orw_tiles_per_seq
        orw_spec = pl.BlockSpec((tm, RWKV_WIDTH), lambda i: (i % tps, i // tps))
    else:
        orw_spec = row(orw)
    consts = (lnw, lnb, bd, wo1, wo2, wo3, n2, wg, wu, wd, wpp, pn, wpg, nf)
    return pl.pallas_call(
        functools.partial(_dense_out_kernel, final=final),
        grid=(n // tm,),
        in_specs=[row(x), row(of), orw_spec, row(gb), row(oret), pe_spec] + [_resident(c.shape) for c in consts],
        out_specs=row(x),
        out_shape=jax.ShapeDtypeStruct((n, d), F32),
        compiler_params=_cparams(("parallel",)),
        name="dense_out",
    )(x, of, orw, gb, oret, pe, *consts)


def _block_diag_ones(width):
    idx = np.arange(width) // HEAD_DIM
    return jnp.asarray(idx[:, None] == idx[None, :], F32)


def _group(x, pe_all, hist, lw, tables, layer, final):
    (n1, wg1, wu1, wd1, nm, win, wkvt, bfp, mu, w0, a0, k_k, k_a, r_k, wup, aup, gup, lnw, lnb,
     wo1, wo2, wo3, n2, wg2, wu2, wd2, wpp, pn, wpg, nf) = lw
    b, t, d = x.shape
    n = b * t
    bd256, bd128 = _block_diag_ones(RWKV_WIDTH), _block_diag_ones(LANES)
    tiles_per_seq = t // DENSE_TM if t % DENSE_TM == 0 else 0
    x2, q, k, v, fg, lf, zr, zt = _dense_in(x.reshape(n, d), n1, wg1, wu1, wd1, nm, win, wkvt, bfp, tiles_per_seq)
    r3 = lambda a: a.reshape(b, t, a.shape[-1])
    lf3 = r3(lf)
    logf = lf3[:, :, :FOX_HEADS]
    zr3 = r3(zr)
    if hist is None:
        prev = jnp.zeros((b, 1, RW_PAD), F32)
        s0 = jnp.zeros((HEAD_DIM, HEAD_DIM, RWKV_HEADS * b), F32)
        r0 = jnp.zeros((b, RET_HEADS // 2, LANES, LANES), F32)
        blk = RET_BLOCK
    else:
        kc, vc, clf, s0, prev, r0 = hist
        blk = t
    cos, sin, dmat, dq, ds, gc = tables
    scan_in, post_in = _rwkv_pre(zr3, prev, mu, w0, a0, k_k, k_a, r_k, wup, aup, gup, bd256)
    if hist is None:
        lft = logf.transpose(0, 2, 1)
        ccol, crow = _cumsum(lf3, lft)
        assert tiles_per_seq, "prompt length must be a multiple of the dense tile"
        of = _fox_prompt(r3(q), k, v, r3(fg), ccol, crow)
        fk, fv = (a.reshape(b, FOX_HEADS, HEAD_DIM, t) for a in (k, v))
    else:
        lf_all = jnp.concatenate([jnp.pad(clf, ((0, 0), (0, 0), (0, LANES - FOX_HEADS))), lf3], axis=1)
        lft = lf_all[:, :, :FOX_HEADS].transpose(0, 2, 1)
        ccol, crow = _cumsum(lf_all, lft)
        assert not tiles_per_seq
        of = _fox_sample(r3(q), r3(k), r3(v), r3(fg), kc, vc, layer, ccol, crow)
        fk, fv = (a.reshape(b, t, FOX_HEADS, HEAD_DIM) for a in (k, v))
    oret, r_fin = _retention(r3(zt), cos, sin, dmat, dq, ds, gc, r0, bd128, blk)
    o_tm, s_fin = _rwkv_scan(scan_in.reshape(t, b, 6 * RWKV_WIDTH), s0, of)
    out_tiles_per_seq = t // DENSE_OUT_TM if t % DENSE_OUT_TM == 0 else 0
    if out_tiles_per_seq:
        orw = o_tm.reshape(t, b * RWKV_WIDTH)
    else:
        orw = o_tm.transpose(1, 0, 2).reshape(n, RWKV_WIDTH)
    xo = _dense_out(x2, of.reshape(n, FOX_WIDTH), orw, post_in.reshape(n, 2 * RWKV_WIDTH), oret.reshape(n, RET_WIDTH),
                    pe_all.reshape(pe_all.shape[0], n, pe_all.shape[-1]), lnw, lnb, bd256, wo1, wo2, wo3, n2, wg2, wu2,
                    wd2, wpp, pn, wpg, nf, final, out_tiles_per_seq, layer)
    s_new = s_fin.reshape(HEAD_DIM, HEAD_DIM, RWKV_HEADS, b).transpose(3, 2, 1, 0)
    r_new = jnp.stack([r_fin[:, hh // 2, (hh % 2) * HEAD_DIM:(hh % 2 + 1) * HEAD_DIM,
                             (hh % 2) * HEAD_DIM:(hh % 2 + 1) * HEAD_DIM] for hh in range(RET_HEADS)], axis=1)
    state = (fk, fv, logf, s_new, zr3[:, t - 1:t, :], r_new)
    return xo.reshape(b, t, d), state


def kernel(x_prompt, x_sample, cache_fox_k, cache_fox_v, cache_fox_logf, state_rwkv, state_rwkv_shift, state_ret,
           p_prompt, p_sample, norm_ffn1, w_ffn1_gate, w_ffn1_up, w_ffn1_down, norm_mix, w_in, fox_forget_bias,
           rwkv_mu, rwkv_w0, rwkv_w_up, rwkv_a0, rwkv_a_up, rwkv_g_up, rwkv_k_k, rwkv_k_a, rwkv_r_k, rwkv_ln_w,
           rwkv_ln_b, w_out, norm_ffn2, w_ffn2_gate, w_ffn2_up, w_ffn2_down, w_ple_proj, ple_norm, w_ple_gate,
           norm_final):
    depth, d, _ = w_in.shape
    bp, tp, _ = x_prompt.shape
    bs, ts, _ = x_sample.shape
    past = cache_fox_k.shape[2]
    assert RWKV_HEADS * bp == LANES and RWKV_HEADS * bs == LANES, "rwkv scan puts (head, batch) on the 128 lanes"
    assert tp % RET_BLOCK == 0 and tp % FOX_BQ == 0 and (bp * tp) % DENSE_OUT_TM == 0 and (bs * ts) % DENSE_OUT_TM == 0

    fw, rw = FOX_WIDTH, RWKV_WIDTH
    o_f = 3 * fw
    o_g = o_f + FOX_HEADS
    o_rw = o_g + fw
    o_ret = o_rw + 3 * rw + 3 * RWKV_RANK
    zpad = lambda w: jnp.zeros((depth, d, w), F32)
    r_ = RWKV_RANK
    rw_cols = 3 * rw + 3 * r_

    def regroup(a):
        parts = [a[..., 0:rw], a[..., rw + r_:2 * rw + r_], a[..., 2 * rw + r_:3 * rw + r_], a[..., rw:rw + r_],
                 a[..., 3 * rw + r_:rw_cols], jnp.zeros(a.shape[:-1] + (RW_PAD - rw_cols,), a.dtype)]
        return jnp.concatenate(parts, axis=-1)

    def ungroup(a):
        parts = [a[..., 0:rw], a[..., 3 * rw:3 * rw + r_], a[..., rw:3 * rw], a[..., 3 * rw + r_:rw_cols]]
        return jnp.concatenate(parts, axis=-1)

    win_all = jnp.concatenate([w_in[:, :, 0:o_f], w_in[:, :, o_g:o_rw], w_in[:, :, o_f:o_g], zpad(LANES - FOX_HEADS),
                               regroup(w_in[:, :, o_rw:o_ret]), w_in[:, :, o_ret:]], axis=2).astype(BF16)
    wkvt = win_all[:, :, C_K:C_G].transpose(0, 2, 1)
    bfp = jnp.pad(fox_forget_bias, ((0, 0), (0, LANES - FOX_HEADS)))[:, None, :]
    mu_p = regroup(rwkv_mu)[:, None, :]
    lowrank = lambda w, slot: jnp.pad(w, ((0, 0), (slot * RWKV_RANK, LANES - (slot + 1) * RWKV_RANK), (0, 0))).astype(BF16)
    wup, aup, gup = lowrank(rwkv_w_up, 0), lowrank(rwkv_a_up, 1), lowrank(rwkv_g_up, 2)
    vec = lambda a: a.reshape(depth, 1, -1)
    bf = lambda a: a.astype(BF16)
    wg1, wu1, wd1 = bf(w_ffn1_gate), bf(w_ffn1_up), bf(w_ffn1_down)
    wg2, wu2, wd2 = bf(w_ffn2_gate), bf(w_ffn2_up), bf(w_ffn2_down)
    wo = bf(w_out)
    wpp, wpg = bf(w_ple_proj), bf(w_ple_gate)
    nf = norm_final.reshape(1, -1)

    def layer_weights(l):
        return (vec(norm_ffn1)[l], wg1[l], wu1[l], wd1[l], vec(norm_mix)[l], win_all[l], wkvt[l], bfp[l], mu_p[l],
                vec(rwkv_w0)[l], vec(rwkv_a0)[l], vec(rwkv_k_k)[l], vec(rwkv_k_a)[l], vec(rwkv_r_k)[l],
                wup[l], aup[l], gup[l], vec(rwkv_ln_w)[l], vec(rwkv_ln_b)[l],
                wo[l, 0:fw], wo[l, fw:fw + rw], wo[l, fw + rw:], vec(norm_ffn2)[l], wg2[l], wu2[l], wd2[l],
                wpp[l], vec(ple_norm)[l], wpg[l], nf)

    kc = cache_fox_k.transpose(0, 1, 3, 4, 2)
    vc = cache_fox_v.transpose(0, 1, 3, 4, 2)
    s0_all = state_rwkv.transpose(0, 4, 3, 2, 1).reshape(depth, HEAD_DIM, HEAD_DIM, RWKV_HEADS * bs)
    prev_all = regroup(state_rwkv_shift)
    eye2 = jnp.eye(2, dtype=F32)
    r0_all = jnp.einsum('lbpqde,qr->lbpqdre', state_ret.reshape(depth, bs, RET_HEADS // 2, 2, HEAD_DIM, HEAD_DIM),
                        eye2).reshape(depth, bs, RET_HEADS // 2, LANES, LANES)

    tab_p = _retention_tables(RET_BLOCK, RET_CHUNK, tp, 0)
    tab_s = _retention_tables(ts, ts, ts, past)

    xp, xs = x_prompt, x_sample
    p_states, s_states = [], []
    for l in range(depth):
        lw = layer_weights(l)
        final = l == depth - 1
        xp, sp = _group(xp, p_prompt, None, lw, tab_p, l, final)
        hist = (kc, vc, cache_fox_logf[l], s0_all[l], prev_all[l], r0_all[l])
        xs, ss = _group(xs, p_sample, hist, lw, tab_s, l, final)
        p_states.append(sp)
        s_states.append(ss)

    def stacked(states, kv_transposed):
        cols = [jnp.stack([s[i] for s in states]) for i in range(6)]
        if kv_transposed:
            cols[0], cols[1] = cols[0].transpose(0, 1, 4, 2, 3), cols[1].transpose(0, 1, 4, 2, 3)
        cols[4] = ungroup(cols[4])
        return cols

    return (xp, xs, *stacked(p_states, True), *stacked(s_states, False))
```

```python
import functools

import jax
import jax.numpy as jnp
import numpy as np
from jax import lax
from jax.experimental import pallas as pl
from jax.experimental.pallas import tpu as pltpu

F32 = jnp.float32
BF16 = jnp.bfloat16

HEAD_DIM = 64
FOX_HEADS = 8
RWKV_HEADS = 4
RET_HEADS = 4
FOX_WIDTH = FOX_HEADS * HEAD_DIM
RWKV_WIDTH = RWKV_HEADS * HEAD_DIM
RET_WIDTH = RET_HEADS * HEAD_DIM
RWKV_RANK = 32
RET_CHUNK = 64
ROPE_BASE = 10000.0
NORM_EPS = 1e-6
RWKV_GN_EPS = 64e-5
RET_GN_EPS = 1e-6

LANES = 128
VMEM_LIMIT_BYTES = 58 * 1024 * 1024
DENSE_TM = 256
DENSE_OUT_TM = 512
FFN_CHUNK = 1408
FOX_BQ = 256
CUM_BLK = 256
RWKV_TT = 256
SCAN_TC = 16
SCAN_K_UNROLL = 8
SCAN_PAIRS = 3
F_R, F_W, F_K, F_V, F_KK, F_KKA = range(6)
RET_BLOCK = 256

C_Q, C_K, C_V, C_G, C_F, C_RW, C_RET, C_END = 0, 512, 1024, 1536, 2048, 2176, 3072, 4096
RW_PAD = C_RET - C_RW


def _cparams(sem):
    return pltpu.CompilerParams(dimension_semantics=sem, vmem_limit_bytes=VMEM_LIMIT_BYTES)


def _resident(shape):
    nd = len(shape)
    return pl.BlockSpec(shape, lambda *_: (0,) * nd, pipeline_mode=pl.Buffered(1))


def _rms(x, g):
    return x * lax.rsqrt(jnp.mean(x * x, axis=-1, keepdims=True) + NORM_EPS) * g


def _sigmoid(x):
    return 1.0 / (1.0 + jnp.exp(-x))


def _log1pexp_neg_abs(x):
    return jnp.log(1.0 + jnp.exp(-jnp.abs(x)))


def _split2(x):
    hi = x.astype(BF16).astype(F32)
    lo = (x - hi).astype(BF16).astype(F32)
    return hi, lo


def _seg_sum(x, bd):
    hi, lo = _split2(x)
    return jnp.dot(hi, bd, preferred_element_type=F32) + jnp.dot(lo, bd, preferred_element_type=F32)


def _head_norm(o, bd, eps):
    mu = _seg_sum(o, bd) * (1.0 / HEAD_DIM)
    d = o - mu
    var = _seg_sum(d * d, bd) * (1.0 / HEAD_DIM)
    return d * lax.rsqrt(var + eps)


def _swiglu(h, wg_ref, wu_ref, wd_ref):
    d_ff = wg_ref.shape[1]
    acc = None
    for c0 in range(0, d_ff, FFN_CHUNK):
        g = jnp.dot(h, wg_ref[:, c0:c0 + FFN_CHUNK], preferred_element_type=F32)
        u = jnp.dot(h, wu_ref[:, c0:c0 + FFN_CHUNK], preferred_element_type=F32)
        a = (g * _sigmoid(g) * u).astype(BF16)
        d = jnp.dot(a, wd_ref[c0:c0 + FFN_CHUNK, :], preferred_element_type=F32)
        acc = d if acc is None else acc + d
    return acc


def _dense_in_kernel(x_ref, n1_ref, wg_ref, wu_ref, wd_ref, nm_ref, win_ref, wkvt_ref, bf_ref,
                     xo_ref, q_ref, k_ref, v_ref, g_ref, lf_ref, zr_ref, zt_ref, *, kv_transposed):
    x = x_ref[...]
    x = x + 0.5 * _swiglu(_rms(x, n1_ref[...]).astype(BF16), wg_ref, wu_ref, wd_ref)
    xo_ref[...] = x
    h = _rms(x, nm_ref[...]).astype(BF16)
    if kv_transposed:
        zq = jnp.dot(h, win_ref[:, C_Q:C_K], preferred_element_type=F32)
        kvt = lax.dot_general(wkvt_ref[...], h, (((1,), (1,)), ((), ())), preferred_element_type=F32)
        k_ref[0] = kvt[:FOX_WIDTH]
        v_ref[0] = kvt[FOX_WIDTH:]
        z = jnp.dot(h, win_ref[:, C_G:C_END], preferred_element_type=F32)
        off = C_G
    else:
        z = jnp.dot(h, win_ref[...], preferred_element_type=F32)
        zq = z[:, C_Q:C_K]
        k_ref[...] = z[:, C_K:C_V]
        v_ref[...] = z[:, C_V:C_G]
        off = 0
    q_ref[...] = (zq * (HEAD_DIM ** -0.5)).astype(BF16)
    g_ref[...] = z[:, C_G - off:C_F - off]
    f = z[:, C_F - off:C_RW - off] + bf_ref[...]
    lf_ref[...] = jnp.minimum(f, 0.0) - _log1pexp_neg_abs(f)
    zr_ref[...] = z[:, C_RW - off:C_RET - off]
    zt_ref[...] = z[:, C_RET - off:C_END - off]


def _dense_in(x, n1, wg, wu, wd, nm, win, wkvt, bfp, tiles_per_seq):
    n, d = x.shape
    tm = DENSE_TM
    row = lambda w: pl.BlockSpec((tm, w), lambda i: (i, 0))
    outs = [(d, F32), (FOX_WIDTH, BF16), (FOX_WIDTH, F32), (FOX_WIDTH, F32), (FOX_WIDTH, F32),
            (LANES, F32), (RW_PAD, F32), (C_END - C_RET, F32)]
    out_specs = [row(w) for w, _ in outs]
    out_shape = [jax.ShapeDtypeStruct((n, w), dt) for w, dt in outs]
    if tiles_per_seq:
        tps = tiles_per_seq
        for j in (2, 3):
            out_specs[j] = pl.BlockSpec((1, FOX_WIDTH, tm), lambda i: (i // tps, 0, i % tps))
            out_shape[j] = jax.ShapeDtypeStruct((n // (tps * tm), FOX_WIDTH, tps * tm), F32)
    return pl.pallas_call(
        functools.partial(_dense_in_kernel, kv_transposed=bool(tiles_per_seq)),
        grid=(n // tm,),
        in_specs=[row(d), _resident(n1.shape), _resident(wg.shape), _resident(wu.shape), _resident(wd.shape),
                  _resident(nm.shape), _resident(win.shape), _resident(wkvt.shape), _resident(bfp.shape)],
        out_specs=out_specs,
        out_shape=out_shape,
        compiler_params=_cparams(("parallel",)),
        name="dense_in",
    )(x, n1, wg, wu, wd, nm, win, wkvt, bfp)


def _cumsum_kernel(lf_ref, lft_ref, ccol_ref, crow_ref, *, t_len):
    nh = lft_ref.shape[1]
    r = lax.broadcasted_iota(jnp.int32, (CUM_BLK, CUM_BLK), 0)
    c = lax.broadcasted_iota(jnp.int32, (CUM_BLK, CUM_BLK), 1)
    tri_l = (r >= c).astype(F32)
    tri_u = (r <= c).astype(F32)
    carry_c = jnp.zeros((1, LANES), F32)
    carry_r = jnp.zeros((nh, 1), F32)
    for s in range(0, t_len, CUM_BLK):
        n = min(CUM_BLK, t_len - s)
        x = lf_ref[0, s:s + n, :]
        hi = x.astype(BF16).astype(F32)
        r1 = x - hi
        mid = r1.astype(BF16).astype(F32)
        lo = r1 - mid
        tl = tri_l[:n, :n]
        cc = (jnp.dot(tl, hi, preferred_element_type=F32) + jnp.dot(tl, mid, preferred_element_type=F32)
              + jnp.dot(tl, lo, preferred_element_type=F32)) + carry_c
        ccol_ref[0, s:s + n, :] = cc
        carry_c = cc[n - 1:n, :]
        y = lft_ref[0, :, s:s + n]
        hi = y.astype(BF16).astype(F32)
        r1 = y - hi
        mid = r1.astype(BF16).astype(F32)
        lo = r1 - mid
        tu = tri_u[:n, :n]
        cr = (jnp.dot(hi, tu, preferred_element_type=F32) + jnp.dot(mid, tu, preferred_element_type=F32)
              + jnp.dot(lo, tu, preferred_element_type=F32)) + carry_r
        crow_ref[0, :, s:s + n] = cr
        carry_r = cr[:, n - 1:n]


def _cumsum(lf3, lft):
    b, t, _ = lf3.shape
    nh = lft.shape[1]
    return pl.pallas_call(
        functools.partial(_cumsum_kernel, t_len=t),
        grid=(b,),
        in_specs=[pl.BlockSpec((1, t, LANES), lambda i: (i, 0, 0)), pl.BlockSpec((1, nh, t), lambda i: (i, 0, 0))],
        out_specs=[pl.BlockSpec((1, t, LANES), lambda i: (i, 0, 0)), pl.BlockSpec((1, nh, t), lambda i: (i, 0, 0))],
        out_shape=[jax.ShapeDtypeStruct((b, t, LANES), F32), jax.ShapeDtypeStruct((b, nh, t), F32)],
        compiler_params=_cparams(("parallel",)),
        name="logf_cumsum",
    )(lf3, lft)


def _qk(q, k):
    return lax.dot_general(q, k, (((1,), (1,)), ((), ())), preferred_element_type=F32)


def _softmax_pv(s_list, v_list, v_transposed):
    m = None
    for s in s_list:
        mi = jnp.max(s, axis=1, keepdims=True)
        m = mi if m is None else jnp.maximum(m, mi)
    l = None
    o = None
    for s, v, vt in zip(s_list, v_list, v_transposed):
        p = jnp.exp(s - m)
        li = jnp.sum(p, axis=1, keepdims=True)
        oi = _qk(p.astype(BF16), v) if vt else jnp.dot(p.astype(BF16), v, preferred_element_type=F32)
        l = li if l is None else l + li
        o = oi if o is None else o + oi
    return o / l


def _fox_prompt_kernel(q_ref, k_ref, v_ref, g_ref, ccol_ref, crow_ref, o_ref, *, t_len):
    pair = pl.program_id(1)
    lane = lax.broadcasted_iota(jnp.int32, (1, LANES), 1)
    low = lane < HEAD_DIM
    kb = k_ref[0].astype(BF16)
    vb = v_ref[0].astype(BF16)
    rr = lax.broadcasted_iota(jnp.int32, (FOX_BQ, FOX_BQ), 0)
    cc = lax.broadcasted_iota(jnp.int32, (FOX_BQ, FOX_BQ), 1)
    causal = rr >= cc
    for s0 in range(0, t_len, FOX_BQ):
        e = s0 + FOX_BQ
        q = q_ref[0, s0:e, :]
        ccol = ccol_ref[0, s0:e, :]
        zq = jnp.zeros_like(q)
        q2 = jnp.concatenate([jnp.where(low, q, zq), jnp.where(low, zq, q)], axis=0)
        sd2 = jnp.dot(q2, kb[:, s0:e], preferred_element_type=F32)
        sp2 = jnp.dot(q2, kb[:, :s0], preferred_element_type=F32) if s0 > 0 else None
        p_diag, p_past, denom = [], [], []
        for hh in range(2):
            h = 2 * pair + hh
            rows = slice(hh * FOX_BQ, (hh + 1) * FOX_BQ)
            cq = jnp.sum(jnp.where(lane == h, ccol, 0.0), axis=1, keepdims=True)
            ck = crow_ref[0, pl.ds(h, 1), :]
            sd = jnp.where(causal, sd2[rows] + cq - ck[:, s0:e], -jnp.inf)
            m = jnp.max(sd, axis=1, keepdims=True)
            if s0 > 0:
                sp = sp2[rows] + cq - ck[:, :s0]
                m = jnp.maximum(m, jnp.max(sp, axis=1, keepdims=True))
                pp = jnp.exp(sp - m)
                p_past.append(pp.astype(BF16))
            pd = jnp.exp(sd - m)
            p_diag.append(pd.astype(BF16))
            l = jnp.sum(pd, axis=1, keepdims=True)
            denom.append(l + jnp.sum(pp, axis=1, keepdims=True) if s0 > 0 else l)
        o2 = _qk(jnp.concatenate(p_diag, axis=0), vb[:, s0:e])
        if s0 > 0:
            o2 = o2 + _qk(jnp.concatenate(p_past, axis=0), vb[:, :s0])
        o = jnp.where(low, o2[:FOX_BQ] / denom[0], o2[FOX_BQ:] / denom[1]) * _sigmoid(g_ref[0, s0:e, :])
        o_ref[0, s0:e, :] = o.astype(o_ref.dtype)


def _fox_prompt(q3, kt, vt, g3, ccol, crow):
    b, t, w = q3.shape
    npair = w // LANES
    nh = crow.shape[1]
    blk = lambda: pl.BlockSpec((1, t, LANES), lambda i, p: (i, 0, p))
    blk_t = lambda: pl.BlockSpec((1, LANES, t), lambda i, p: (i, p, 0))
    return pl.pallas_call(
        functools.partial(_fox_prompt_kernel, t_len=t),
        grid=(b, npair),
        in_specs=[blk(), blk_t(), blk_t(), blk(),
                  pl.BlockSpec((1, t, LANES), lambda i, p: (i, 0, 0)),
                  pl.BlockSpec((1, nh, t), lambda i, p: (i, 0, 0))],
        out_specs=blk(),
        out_shape=jax.ShapeDtypeStruct((b, t, w), BF16),
        compiler_params=_cparams(("parallel", "parallel")),
        name="fox_prompt",
    )(q3, kt, vt, g3, ccol, crow)


def _fox_sample_kernel(q_ref, kn_ref, vn_ref, g_ref, kc_ref, vc_ref, ccol_ref, crow_ref, o_ref, *, past, n_new):
    pair = pl.program_id(1)
    lane = lax.broadcasted_iota(jnp.int32, (1, LANES), 1)
    low = lane < HEAD_DIM
    kc = kc_ref[0, 0].reshape(LANES, past).astype(BF16)
    vc = vc_ref[0, 0].reshape(LANES, past).astype(BF16)
    kn = kn_ref[0].astype(BF16)
    vn = vn_ref[0].astype(BF16)
    q = q_ref[0]
    ccol = ccol_ref[0, past:past + n_new, :]
    rr = lax.broadcasted_iota(jnp.int32, (n_new, n_new), 0)
    cc = lax.broadcasted_iota(jnp.int32, (n_new, n_new), 1)
    causal = rr >= cc
    res = []
    for hh in range(2):
        h = 2 * pair + hh
        msk = low if hh == 0 else jnp.logical_not(low)
        qm = jnp.where(msk, q, jnp.zeros_like(q))
        cq = jnp.sum(jnp.where(lane == h, ccol, 0.0), axis=1, keepdims=True)
        ck = crow_ref[0, pl.ds(h, 1), :]
        sc = jnp.dot(qm, kc, preferred_element_type=F32) + cq - ck[:, :past]
        sn = _qk(qm, kn) + cq - ck[:, past:past + n_new]
        sn = jnp.where(causal, sn, -jnp.inf)
        res.append(_softmax_pv([sc, sn], [vc, vn], [True, False]))
    o = jnp.where(low, res[0], res[1]) * _sigmoid(g_ref[0])
    o_ref[0] = o.astype(o_ref.dtype)


def _fox_sample(q3, k3, v3, g3, kc, vc, layer, ccol, crow):
    b, n_new, w = q3.shape
    past = kc.shape[4]
    npair = w // LANES
    nh = crow.shape[1]
    tot = past + n_new
    blk = lambda: pl.BlockSpec((1, n_new, LANES), lambda i, p: (i, 0, p))
    cache = lambda: pl.BlockSpec((1, 1, 2, HEAD_DIM, past), lambda i, p: (layer, i, p, 0, 0))
    return pl.pallas_call(
        functools.partial(_fox_sample_kernel, past=past, n_new=n_new),
        grid=(b, npair),
        in_specs=[blk(), blk(), blk(), blk(), cache(), cache(),
                  pl.BlockSpec((1, tot, LANES), lambda i, p: (i, 0, 0)),
                  pl.BlockSpec((1, nh, tot), lambda i, p: (i, 0, 0))],
        out_specs=blk(),
        out_shape=jax.ShapeDtypeStruct((b, n_new, w), BF16),
        compiler_params=_cparams(("parallel", "parallel")),
        name="fox_sample",
    )(q3, k3, v3, g3, kc, vc, ccol, crow)


def _rwkv_pre_kernel(z_ref, zp_ref, prev_ref, mu_ref, w0_ref, a0_ref, kk_ref, ka_ref, rk_ref,
                     wup_ref, aup_ref, gup_ref, bd_ref, scan_ref, post_ref):
    i = pl.program_id(1)
    z = z_ref[0]
    tt = z.shape[0]
    p_in = jnp.where(i == 0, prev_ref[0], zp_ref[0, 7:8, :])
    row = lax.broadcasted_iota(jnp.int32, (tt, 1), 0)
    shifted = jnp.where(row == 0, p_in, pltpu.roll(z, 1, axis=0))
    zs = z + mu_ref[...] * (shifted - z)
    w_ = RWKV_WIDTH
    r, k, v, low = zs[:, 0:w_], zs[:, w_:2 * w_], zs[:, 2 * w_:3 * w_], zs[:, 3 * w_:]
    bd = bd_ref[...]
    y = w0_ref[...] + jnp.dot(jnp.tanh(low).astype(BF16), wup_ref[...], preferred_element_type=F32)
    sp = jnp.maximum(-y, 0.0) + _log1pexp_neg_abs(y)
    decay = jnp.exp(-jnp.exp(-sp - 0.5))
    a = _sigmoid(a0_ref[...] + jnp.dot(low.astype(BF16), aup_ref[...], preferred_element_type=F32))
    g = jnp.dot(_sigmoid(low).astype(BF16), gup_ref[...], preferred_element_type=F32)
    kk = k * kk_ref[...]
    kk = kk * lax.rsqrt(jnp.maximum(_seg_sum(kk * kk, bd), 1e-24))
    k2 = k * (1.0 + (a - 1.0) * ka_ref[...])
    bonus = _seg_sum(r * k2 * rk_ref[...], bd) * v
    fields = {F_R: r, F_W: decay, F_K: k2, F_V: v, F_KK: kk, F_KKA: kk * a}
    per_head = 2 * SCAN_PAIRS * HEAD_DIM
    for h in range(RWKV_HEADS):
        for f, val in fields.items():
            c0 = h * per_head + f * HEAD_DIM
            scan_ref[:, c0:c0 + HEAD_DIM] = val[:, h * HEAD_DIM:(h + 1) * HEAD_DIM]
    post_ref[0, :, 0:w_] = g
    post_ref[0, :, w_:2 * w_] = bonus


def _rwkv_pre(zr3, prev, mu, w0, a0, k_k, k_a, r_k, wup, aup, gup, bd):
    b, t, w = zr3.shape
    tt = min(RWKV_TT, t)
    full = lambda a: pl.BlockSpec(a.shape, lambda i, j: (0,) * a.ndim)
    return pl.pallas_call(
        _rwkv_pre_kernel,
        grid=(b, t // tt),
        in_specs=[pl.BlockSpec((1, tt, w), lambda i, j: (i, j, 0)),
                  pl.BlockSpec((1, 8, w), lambda i, j: (i, jnp.maximum(j * (tt // 8) - 1, 0), 0)),
                  pl.BlockSpec((1, 1, w), lambda i, j: (i, 0, 0)),
                  full(mu), full(w0), full(a0), full(k_k), full(k_a), full(r_k), full(wup), full(aup), full(gup),
                  full(bd)],
        out_specs=[pl.BlockSpec((tt, 6 * RWKV_WIDTH), lambda i, j: (j, i)),
                   pl.BlockSpec((1, tt, 2 * RWKV_WIDTH), lambda i, j: (i, j, 0))],
        out_shape=[jax.ShapeDtypeStruct((t, b * 6 * RWKV_WIDTH), F32), jax.ShapeDtypeStruct((b, t, 2 * RWKV_WIDTH), F32)],
        compiler_params=_cparams(("parallel", "parallel")),
        name="rwkv_pre",
    )(zr3, zr3, prev, mu, w0, a0, k_k, k_a, r_k, wup, aup, gup, bd)


def _rwkv_scan_kernel(*refs):
    x_ref, s0_ref, _, o_ref, s_ref, x_scr, o_scr = refs
    i = pl.program_id(0)
    tc, nb = x_ref.shape[0], x_ref.shape[1]
    ng = HEAD_DIM // 8

    @pl.when(i == 0)
    def _():
        s_ref[...] = s0_ref[...]

    for t in range(tc):
        for fp in range(SCAN_PAIRS):
            cols = [(h * SCAN_PAIRS + fp) * LANES for h in range(RWKV_HEADS)]
            y = jnp.concatenate([x_ref[t, :, c:c + LANES] for c in cols], axis=0)
            yt = y.T
            x_scr[t, 2 * fp] = yt[0:HEAD_DIM]
            x_scr[t, 2 * fp + 1] = yt[HEAD_DIM:2 * HEAD_DIM]

    def step(t, carry):
        def dot_kk(k, acc):
            kk = x_scr[t, F_KK, pl.ds(k, 1), :]
            return tuple(acc[g] + s_ref[k, g * 8:(g + 1) * 8, :] * kk for g in range(ng))

        zero = tuple(jnp.zeros((8, LANES), F32) for _ in range(ng))
        skk = lax.fori_loop(0, HEAD_DIM, dot_kk, zero, unroll=SCAN_K_UNROLL)
        vv = tuple(x_scr[t, F_V, g * 8:(g + 1) * 8, :] for g in range(ng))

        def update(k, acc):
            w = x_scr[t, F_W, pl.ds(k, 1), :]
            kka = x_scr[t, F_KKA, pl.ds(k, 1), :]
            kx = x_scr[t, F_K, pl.ds(k, 1), :]
            r = x_scr[t, F_R, pl.ds(k, 1), :]
            out = []
            for g in range(ng):
                sn = s_ref[k, g * 8:(g + 1) * 8, :] * w - skk[g] * kka + vv[g] * kx
                s_ref[k, g * 8:(g + 1) * 8, :] = sn
                out.append(acc[g] + sn * r)
            return tuple(out)

        o = lax.fori_loop(0, HEAD_DIM, update, zero, unroll=SCAN_K_UNROLL)
        for g in range(ng):
            o_scr[t, g * 8:(g + 1) * 8, :] = o[g]
        return carry

    lax.fori_loop(0, tc, step, 0)

    low = lax.broadcasted_iota(jnp.int32, (1, LANES), 1) < HEAD_DIM
    for t in range(0, tc, 2):
        z = jnp.concatenate([o_scr[t], o_scr[t + 1]], axis=0).T
        zr = pltpu.roll(z, HEAD_DIM, axis=1)
        for p in range(RWKV_HEADS // 2):
            ev = slice(2 * p * nb, (2 * p + 1) * nb)
            od = slice((2 * p + 1) * nb, (2 * p + 2) * nb)
            o_ref[t, :, p * LANES:(p + 1) * LANES] = jnp.where(low, z[ev], zr[od])
            o_ref[t + 1, :, p * LANES:(p + 1) * LANES] = jnp.where(low, zr[ev], z[od])


def _rwkv_scan(scan_in, s0, after):
    t, b, w = scan_in.shape
    tc = min(SCAN_TC, t)
    state = pl.BlockSpec((HEAD_DIM, HEAD_DIM, LANES), lambda i: (0, 0, 0))
    return pl.pallas_call(
        _rwkv_scan_kernel,
        grid=(t // tc,),
        in_specs=[pl.BlockSpec((tc, b, w), lambda i: (i, 0, 0)), state,
                  pl.BlockSpec((1, 16, LANES), lambda i: (0, 0, 0))],
        out_specs=[pl.BlockSpec((tc, b, RWKV_WIDTH), lambda i: (i, 0, 0)), state],
        out_shape=[jax.ShapeDtypeStruct((t, b, RWKV_WIDTH), F32),
                   jax.ShapeDtypeStruct((HEAD_DIM, HEAD_DIM, LANES), F32)],
        scratch_shapes=[pltpu.VMEM((tc, 2 * SCAN_PAIRS, HEAD_DIM, LANES), F32), pltpu.VMEM((tc, HEAD_DIM, LANES), F32)],
        compiler_params=_cparams(("arbitrary",)),
        name="rwkv_scan",
    )(scan_in, s0, after)


def _retention_kernel(z_ref, cos_ref, sin_ref, d_ref, dq_ref, ds_ref, gc_ref, r0_ref, bd_ref,
                      o_ref, r_scr):
    i = pl.program_id(1)

    @pl.when(i == 0)
    def _():
        r_scr[...] = r0_ref[...]

    cos = cos_ref[...]
    sin = sin_ref[...]
    lane = lax.broadcasted_iota(jnp.int32, (1, LANES), 1)
    low = lane < HEAD_DIM
    first_half = (lane % HEAD_DIM) < (HEAD_DIM // 2)
    rr = lax.broadcasted_iota(jnp.int32, (LANES, LANES), 0)
    cc = lax.broadcasted_iota(jnp.int32, (LANES, LANES), 1)
    same_head = (rr < HEAD_DIM) == (cc < HEAD_DIM)
    bd = bd_ref[...]
    w_ = RET_WIDTH

    def rot(x):
        swapped = jnp.where(first_half, pltpu.roll(x, LANES - HEAD_DIM // 2, axis=1), pltpu.roll(x, HEAD_DIM // 2, axis=1))
        return x * cos + swapped * sin

    for p in range(RET_HEADS // 2):
        c0 = p * LANES
        qr = rot(z_ref[0, :, c0:c0 + LANES])
        kr = rot(z_ref[0, :, w_ + c0:w_ + c0 + LANES]) * (HEAD_DIM ** -0.5)
        vb = z_ref[0, :, 2 * w_ + c0:2 * w_ + c0 + LANES].astype(BF16)
        gate = z_ref[0, :, 3 * w_ + c0:3 * w_ + c0 + LANES]
        qb = qr.astype(BF16)
        kb = kr.astype(BF16)
        r_old = r_scr[0, p]
        o = jnp.dot((qr * dq_ref[p]).astype(BF16), r_old.astype(BF16), preferred_element_type=F32)
        for hh in range(2):
            msk = low if hh == 0 else jnp.logical_not(low)
            qm = jnp.where(msk, qb, jnp.zeros_like(qb))
            pmat = (_qk(qm, kb) * d_ref[2 * p + hh]).astype(BF16)
            o = o + jnp.where(msk, jnp.dot(pmat, vb, preferred_element_type=F32), 0.0)
        kd = (kr * ds_ref[p]).astype(BF16)
        kv = lax.dot_general(kd, vb, (((0,), (0,)), ((), ())), preferred_element_type=F32)
        r_scr[0, p] = r_old * gc_ref[p] + jnp.where(same_head, kv, 0.0)
        out = _head_norm(o, bd, RET_GN_EPS) * (gate * _sigmoid(gate))
        o_ref[0, :, c0:c0 + LANES] = out.astype(o_ref.dtype)


def _retention(zt3, cos, sin, dmat, dq, ds, gc, r0, bd, blk):
    b, t, w = zt3.shape
    npair = RET_HEADS // 2
    full = lambda a: pl.BlockSpec(a.shape, lambda i, j: (0,) * a.ndim)
    return pl.pallas_call(
        _retention_kernel,
        grid=(b, t // blk),
        in_specs=[pl.BlockSpec((1, blk, w), lambda i, j: (i, j, 0)),
                  pl.BlockSpec((blk, LANES), lambda i, j: (j, 0)),
                  pl.BlockSpec((blk, LANES), lambda i, j: (j, 0)),
                  full(dmat), full(dq), full(ds), full(gc),
                  pl.BlockSpec((1, npair, LANES, LANES), lambda i, j: (i, 0, 0, 0)),
                  full(bd)],
        out_specs=[pl.BlockSpec((1, blk, RET_WIDTH), lambda i, j: (i, j, 0)),
                   pl.BlockSpec((1, npair, LANES, LANES), lambda i, j: (i, 0, 0, 0))],
        out_shape=[jax.ShapeDtypeStruct((b, t, RET_WIDTH), BF16),
                   jax.ShapeDtypeStruct((b, npair, LANES, LANES), F32)],
        compiler_params=_cparams(("parallel", "arbitrary")),
        name="retention",
    )(zt3, cos, sin, dmat, dq, ds, gc, r0, bd)


def _retention_tables(blk, chunk, t_len, pos0):
    log_g = jnp.log(1.0 - jnp.power(2.0, -5.0 - jnp.arange(RET_HEADS, dtype=F32)))
    idx = jnp.arange(blk)
    ci = idx // chunk
    diff = (idx[:, None] - idx[None, :]).astype(F32)
    expo = jnp.where(ci[:, None] == ci[None, :], jnp.abs(diff), diff)
    dmat = jnp.exp(log_g[:, None, None] * expo[None])
    dmat = jnp.where((ci[:, None] >= ci[None, :])[None], dmat, 0.0)
    lg_lane = jnp.repeat(log_g, HEAD_DIM).reshape(RET_HEADS // 2, 1, LANES)
    tau = idx.astype(F32)[None, :, None]
    dq = jnp.exp(lg_lane * (tau + 1.0))
    ds = jnp.exp(lg_lane * (blk - 1.0 - tau))
    gc = jnp.exp(lg_lane * float(blk))
    half = HEAD_DIM // 2
    inv = 1.0 / (ROPE_BASE ** (jnp.arange(half, dtype=F32) / half))
    ang = (jnp.arange(t_len) + pos0).astype(F32)[:, None] * inv[None, :]
    cos = jnp.tile(jnp.cos(ang), (1, LANES // half))
    sin = jnp.sin(ang)
    sin = jnp.tile(jnp.concatenate([-sin, sin], axis=1), (1, LANES // HEAD_DIM))
    return cos, sin, dmat, dq, ds, gc


def _dense_out_kernel(x_ref, of_ref, orw_ref, gb_ref, oret_ref, pe_ref, lnw_ref, lnb_ref, bd_ref,
                      wo1_ref, wo2_ref, wo3_ref, n2_ref, wg_ref, wu_ref, wd_ref, wpp_ref, pn_ref, wpg_ref, nf_ref,
                      xo_ref, *, final):
    w_ = RWKV_WIDTH
    hn = _head_norm(orw_ref[...], bd_ref[...], RWKV_GN_EPS)
    orw = ((hn * lnw_ref[...] + lnb_ref[...] + gb_ref[:, w_:2 * w_]) * gb_ref[:, 0:w_]).astype(BF16)
    x = x_ref[...]
    x = x + (jnp.dot(of_ref[...], wo1_ref[...], preferred_element_type=F32)
             + jnp.dot(orw, wo2_ref[...], preferred_element_type=F32)
             + jnp.dot(oret_ref[...], wo3_ref[...], preferred_element_type=F32))
    x = x + 0.5 * _swiglu(_rms(x, n2_ref[...]).astype(BF16), wg_ref, wu_ref, wd_ref)
    e = _rms(jnp.dot(pe_ref[...].astype(BF16), wpp_ref[...], preferred_element_type=F32), pn_ref[...])
    x = x + _sigmoid(jnp.dot(x.astype(BF16), wpg_ref[...], preferred_element_type=F32)) * e
    if final:
        x = _rms(x, nf_ref[...])
    xo_ref[...] = x


def _dense_out(x, of, orw, gb, oret, pe, lnw, lnb, bd, wo1, wo2, wo3, n2, wg, wu, wd, wpp, pn, wpg, nf, final,
               orw_tiles_per_seq, layer):
    n, d = x.shape
    tm = DENSE_OUT_TM
    row = lambda a: pl.BlockSpec((tm, a.shape[1]), lambda i: (i, 0))
    pe_spec = pl.BlockSpec((None, tm, pe.shape[2]), lambda i: (layer, i, 0))
    if orw_tiles_per_seq:
        tps = orw_tiles_per_seq
        orw_spec = pl.BlockSpec((tm, RWKV_WIDTH), lambda i: (i % tps, i // tps))
    else:
        orw_spec = row(orw)
    consts = (lnw, lnb, bd, wo1, wo2, wo3, n2, wg, wu, wd, wpp, pn, wpg, nf)
    return pl.pallas_call(
        functools.partial(_dense_out_kernel, final=final),
        grid=(n // tm,),
        in_specs=[row(x), row(of), orw_spec, row(gb), row(oret), pe_spec] + [_resident(c.shape) for c in consts],
        out_specs=row(x),
        out_shape=jax.ShapeDtypeStruct((n, d), F32),
        compiler_params=_cparams(("parallel",)),
        name="dense_out",
    )(x, of, orw, gb, oret, pe, *consts)


def _block_diag_ones(width):
    idx = np.arange(width) // HEAD_DIM
    return jnp.asarray(idx[:, None] == idx[None, :], F32)


def _group(x, pe_all, hist, lw, tables, layer, final):
    (n1, wg1, wu1, wd1, nm, win, wkvt, bfp, mu, w0, a0, k_k, k_a, r_k, wup, aup, gup, lnw, lnb,
     wo1, wo2, wo3, n2, wg2, wu2, wd2, wpp, pn, wpg, nf) = lw
    b, t, d = x.shape
    n = b * t
    bd256, bd128 = _block_diag_ones(RWKV_WIDTH), _block_diag_ones(LANES)
    tiles_per_seq = t // DENSE_TM if t % DENSE_TM == 0 else 0
    x2, q, k, v, fg, lf, zr, zt = _dense_in(x.reshape(n, d), n1, wg1, wu1, wd1, nm, win, wkvt, bfp, tiles_per_seq)
    r3 = lambda a: a.reshape(b, t, a.shape[-1])
    lf3 = r3(lf)
    logf = lf3[:, :, :FOX_HEADS]
    zr3 = r3(zr)
    if hist is None:
        prev = jnp.zeros((b, 1, RW_PAD), F32)
        s0 = jnp.zeros((HEAD_DIM, HEAD_DIM, RWKV_HEADS * b), F32)
        r0 = jnp.zeros((b, RET_HEADS // 2, LANES, LANES), F32)
        blk = RET_BLOCK
    else:
        kc, vc, clf, s0, prev, r0 = hist
        blk = t
    cos, sin, dmat, dq, ds, gc = tables
    scan_in, post_in = _rwkv_pre(zr3, prev, mu, w0, a0, k_k, k_a, r_k, wup, aup, gup, bd256)
    if hist is None:
        lft = logf.transpose(0, 2, 1)
        ccol, crow = _cumsum(lf3, lft)
        assert tiles_per_seq, "prompt length must be a multiple of the dense tile"
        of = _fox_prompt(r3(q), k, v, r3(fg), ccol, crow)
        fk, fv = (a.reshape(b, FOX_HEADS, HEAD_DIM, t) for a in (k, v))
    else:
        lf_all = jnp.concatenate([jnp.pad(clf, ((0, 0), (0, 0), (0, LANES - FOX_HEADS))), lf3], axis=1)
        lft = lf_all[:, :, :FOX_HEADS].transpose(0, 2, 1)
        ccol, crow = _cumsum(lf_all, lft)
        assert not tiles_per_seq
        of = _fox_sample(r3(q), r3(k), r3(v), r3(fg), kc, vc, layer, ccol, crow)
        fk, fv = (a.reshape(b, t, FOX_HEADS, HEAD_DIM) for a in (k, v))
    oret, r_fin = _retention(r3(zt), cos, sin, dmat, dq, ds, gc, r0, bd128, blk)
    o_tm, s_fin = _rwkv_scan(scan_in.reshape(t, b, 6 * RWKV_WIDTH), s0, of)
    out_tiles_per_seq = t // DENSE_OUT_TM if t % DENSE_OUT_TM == 0 else 0
    if out_tiles_per_seq:
        orw = o_tm.reshape(t, b * RWKV_WIDTH)
    else:
        orw = o_tm.transpose(1, 0, 2).reshape(n, RWKV_WIDTH)
    xo = _dense_out(x2, of.reshape(n, FOX_WIDTH), orw, post_in.reshape(n, 2 * RWKV_WIDTH), oret.reshape(n, RET_WIDTH),
                    pe_all.reshape(pe_all.shape[0], n, pe_all.shape[-1]), lnw, lnb, bd256, wo1, wo2, wo3, n2, wg2, wu2,
                    wd2, wpp, pn, wpg, nf, final, out_tiles_per_seq, layer)
    s_new = s_fin.reshape(HEAD_DIM, HEAD_DIM, RWKV_HEADS, b).transpose(3, 2, 1, 0)
    r_new = jnp.stack([r_fin[:, hh // 2, (hh % 2) * HEAD_DIM:(hh % 2 + 1) * HEAD_DIM,
                             (hh % 2) * HEAD_DIM:(hh % 2 + 1) * HEAD_DIM] for hh in range(RET_HEADS)], axis=1)
    state = (fk, fv, logf, s_new, zr3[:, t - 1:t, :], r_new)
    return xo.reshape(b, t, d), state


def kernel(x_prompt, x_sample, cache_fox_k, cache_fox_v, cache_fox_logf, state_rwkv, state_rwkv_shift, state_ret,
           p_prompt, p_sample, norm_ffn1, w_ffn1_gate, w_ffn1_up, w_ffn1_down, norm_mix, w_in, fox_forget_bias,
           rwkv_mu, rwkv_w0, rwkv_w_up, rwkv_a0, rwkv_a_up, rwkv_g_up, rwkv_k_k, rwkv_k_a, rwkv_r_k, rwkv_ln_w,
           rwkv_ln_b, w_out, norm_ffn2, w_ffn2_gate, w_ffn2_up, w_ffn2_down, w_ple_proj, ple_norm, w_ple_gate,
           norm_final):
    depth, d, _ = w_in.shape
    bp, tp, _ = x_prompt.shape
    bs, ts, _ = x_sample.shape
    past = cache_fox_k.shape[2]
    assert RWKV_HEADS * bp == LANES and RWKV_HEADS * bs == LANES, "rwkv scan puts (head, batch) on the 128 lanes"
    assert tp % RET_BLOCK == 0 and tp % FOX_BQ == 0 and (bp * tp) % DENSE_OUT_TM == 0 and (bs * ts) % DENSE_OUT_TM == 0

    fw, rw = FOX_WIDTH, RWKV_WIDTH
    o_f = 3 * fw
    o_g = o_f + FOX_HEADS
    o_rw = o_g + fw
    o_ret = o_rw + 3 * rw + 3 * RWKV_RANK
    zpad = lambda w: jnp.zeros((depth, d, w), F32)
    r_ = RWKV_RANK
    rw_cols = 3 * rw + 3 * r_

    def regroup(a):
        parts = [a[..., 0:rw], a[..., rw + r_:2 * rw + r_], a[..., 2 * rw + r_:3 * rw + r_], a[..., rw:rw + r_],
                 a[..., 3 * rw + r_:rw_cols], jnp.zeros(a.shape[:-1] + (RW_PAD - rw_cols,), a.dtype)]
        return jnp.concatenate(parts, axis=-1)

    def ungroup(a):
        parts = [a[..., 0:rw], a[..., 3 * rw:3 * rw + r_], a[..., rw:3 * rw], a[..., 3 * rw + r_:rw_cols]]
        return jnp.concatenate(parts, axis=-1)

    win_all = jnp.concatenate([w_in[:, :, 0:o_f], w_in[:, :, o_g:o_rw], w_in[:, :, o_f:o_g], zpad(LANES - FOX_HEADS),
                               regroup(w_in[:, :, o_rw:o_ret]), w_in[:, :, o_ret:]], axis=2).astype(BF16)
    wkvt = win_all[:, :, C_K:C_G].transpose(0, 2, 1)
    bfp = jnp.pad(fox_forget_bias, ((0, 0), (0, LANES - FOX_HEADS)))[:, None, :]
    mu_p = regroup(rwkv_mu)[:, None, :]
    lowrank = lambda w, slot: jnp.pad(w, ((0, 0), (slot * RWKV_RANK, LANES - (slot + 1) * RWKV_RANK), (0, 0))).astype(BF16)
    wup, aup, gup = lowrank(rwkv_w_up, 0), lowrank(rwkv_a_up, 1), lowrank(rwkv_g_up, 2)
    vec = lambda a: a.reshape(depth, 1, -1)
    bf = lambda a: a.astype(BF16)
    wg1, wu1, wd1 = bf(w_ffn1_gate), bf(w_ffn1_up), bf(w_ffn1_down)
    wg2, wu2, wd2 = bf(w_ffn2_gate), bf(w_ffn2_up), bf(w_ffn2_down)
    wo = bf(w_out)
    wpp, wpg = bf(w_ple_proj), bf(w_ple_gate)
    nf = norm_final.reshape(1, -1)

    def layer_weights(l):
        return (vec(norm_ffn1)[l], wg1[l], wu1[l], wd1[l], vec(norm_mix)[l], win_all[l], wkvt[l], bfp[l], mu_p[l],
                vec(rwkv_w0)[l], vec(rwkv_a0)[l], vec(rwkv_k_k)[l], vec(rwkv_k_a)[l], vec(rwkv_r_k)[l],
                wup[l], aup[l], gup[l], vec(rwkv_ln_w)[l], vec(rwkv_ln_b)[l],
                wo[l, 0:fw], wo[l, fw:fw + rw], wo[l, fw + rw:], vec(norm_ffn2)[l], wg2[l], wu2[l], wd2[l],
                wpp[l], vec(ple_norm)[l], wpg[l], nf)

    kc = cache_fox_k.transpose(0, 1, 3, 4, 2)
    vc = cache_fox_v.transpose(0, 1, 3, 4, 2)
    s0_all = state_rwkv.transpose(0, 4, 3, 2, 1).reshape(depth, HEAD_DIM, HEAD_DIM, RWKV_HEADS * bs)
    prev_all = regroup(state_rwkv_shift)
    eye2 = jnp.eye(2, dtype=F32)
    r0_all = jnp.einsum('lbpqde,qr->lbpqdre', state_ret.reshape(depth, bs, RET_HEADS // 2, 2, HEAD_DIM, HEAD_DIM),
                        eye2).reshape(depth, bs, RET_HEADS // 2, LANES, LANES)

    tab_p = _retention_tables(RET_BLOCK, RET_CHUNK, tp, 0)
    tab_s = _retention_tables(ts, ts, ts, past)

    xp, xs = x_prompt, x_sample
    p_states, s_states = [], []
    for l in range(depth):
        lw = layer_weights(l)
        final = l == depth - 1
        xp, sp = _group(xp, p_prompt, None, lw, tab_p, l, final)
        hist = (kc, vc, cache_fox_logf[l], s0_all[l], prev_all[l], r0_all[l])
        xs, ss = _group(xs, p_sample, hist, lw, tab_s, l, final)
        p_states.append(sp)
        s_states.append(ss)

    def stacked(states, kv_transposed):
        cols = [jnp.stack([s[i] for s in states]) for i in range(6)]
        if kv_transposed:
            cols[0], cols[1] = cols[0].transpose(0, 1, 4, 2, 3), cols[1].transpose(0, 1, 4, 2, 3)
        cols[4] = ungroup(cols[4])
        return cols

    return (xp, xs, *stacked(p_states, True), *stacked(s_states, False))
```

```python
import functools

import jax
import jax.numpy as jnp
import numpy as np
from jax import lax
from jax.experimental import pallas as pl
from jax.experimental.pallas import tpu as pltpu

F32 = jnp.float32
BF16 = jnp.bfloat16

HEAD_DIM = 64
FOX_HEADS = 8
RWKV_HEADS = 4
RET_HEADS = 4
FOX_WIDTH = FOX_HEADS * HEAD_DIM
RWKV_WIDTH = RWKV_HEADS * HEAD_DIM
RET_WIDTH = RET_HEADS * HEAD_DIM
RWKV_RANK = 32
RET_CHUNK = 64
ROPE_BASE = 10000.0
NORM_EPS = 1e-6
RWKV_GN_EPS = 64e-5
RET_GN_EPS = 1e-6

LANES = 128
VMEM_LIMIT_BYTES = 58 * 1024 * 1024
DENSE_TM = 256
DENSE_OUT_TM = 512
FFN_CHUNK = 1408
FOX_BQ = 256
CUM_BLK = 256
RWKV_TT = 256
SCAN_TC = 16
SCAN_K_UNROLL = 64
SCAN_PAIRS = 3
F_R, F_W, F_K, F_V, F_KK, F_KKA = range(6)
RET_BLOCK = 256

C_Q, C_K, C_V, C_G, C_F, C_RW, C_RET, C_END = 0, 512, 1024, 1536, 2048, 2176, 3072, 4096
RW_PAD = C_RET - C_RW


def _cparams(sem):
    return pltpu.CompilerParams(dimension_semantics=sem, vmem_limit_bytes=VMEM_LIMIT_BYTES)


def _resident(shape):
    nd = len(shape)
    return pl.BlockSpec(shape, lambda *_: (0,) * nd, pipeline_mode=pl.Buffered(1))


def _rms(x, g):
    return x * lax.rsqrt(jnp.mean(x * x, axis=-1, keepdims=True) + NORM_EPS) * g


def _sigmoid(x):
    return 1.0 / (1.0 + jnp.exp(-x))


def _log1pexp_neg_abs(x):
    return jnp.log(1.0 + jnp.exp(-jnp.abs(x)))


def _split2(x):
    hi = x.astype(BF16).astype(F32)
    lo = (x - hi).astype(BF16).astype(F32)
    return hi, lo


def _seg_sum(x, bd):
    hi, lo = _split2(x)
    return jnp.dot(hi, bd, preferred_element_type=F32) + jnp.dot(lo, bd, preferred_element_type=F32)


def _head_norm(o, bd, eps):
    mu = _seg_sum(o, bd) * (1.0 / HEAD_DIM)
    d = o - mu
    var = _seg_sum(d * d, bd) * (1.0 / HEAD_DIM)
    return d * lax.rsqrt(var + eps)


def _swiglu(h, wg_ref, wu_ref, wd_ref):
    d_ff = wg_ref.shape[1]
    acc = None
    for c0 in range(0, d_ff, FFN_CHUNK):
        g = jnp.dot(h, wg_ref[:, c0:c0 + FFN_CHUNK], preferred_element_type=F32)
        u = jnp.dot(h, wu_ref[:, c0:c0 + FFN_CHUNK], preferred_element_type=F32)
        a = (g * _sigmoid(g) * u).astype(BF16)
        d = jnp.dot(a, wd_ref[c0:c0 + FFN_CHUNK, :], preferred_element_type=F32)
        acc = d if acc is None else acc + d
    return acc


def _dense_in_kernel(x_ref, n1_ref, wg_ref, wu_ref, wd_ref, nm_ref, win_ref, wkvt_ref, bf_ref,
                     xo_ref, q_ref, k_ref, v_ref, g_ref, lf_ref, zr_ref, zt_ref, *, kv_transposed):
    x = x_ref[...]
    x = x + 0.5 * _swiglu(_rms(x, n1_ref[...]).astype(BF16), wg_ref, wu_ref, wd_ref)
    xo_ref[...] = x
    h = _rms(x, nm_ref[...]).astype(BF16)
    if kv_transposed:
        zq = jnp.dot(h, win_ref[:, C_Q:C_K], preferred_element_type=F32)
        kvt = lax.dot_general(wkvt_ref[...], h, (((1,), (1,)), ((), ())), preferred_element_type=F32)
        k_ref[0] = kvt[:FOX_WIDTH]
        v_ref[0] = kvt[FOX_WIDTH:]
        z = jnp.dot(h, win_ref[:, C_G:C_END], preferred_element_type=F32)
        off = C_G
    else:
        z = jnp.dot(h, win_ref[...], preferred_element_type=F32)
        zq = z[:, C_Q:C_K]
        k_ref[...] = z[:, C_K:C_V]
        v_ref[...] = z[:, C_V:C_G]
        off = 0
    q_ref[...] = (zq * (HEAD_DIM ** -0.5)).astype(BF16)
    g_ref[...] = z[:, C_G - off:C_F - off]
    f = z[:, C_F - off:C_RW - off] + bf_ref[...]
    lf_ref[...] = jnp.minimum(f, 0.0) - _log1pexp_neg_abs(f)
    zr_ref[...] = z[:, C_RW - off:C_RET - off]
    zt_ref[...] = z[:, C_RET - off:C_END - off]


def _dense_in(x, n1, wg, wu, wd, nm, win, wkvt, bfp, tiles_per_seq):
    n, d = x.shape
    tm = DENSE_TM
    row = lambda w: pl.BlockSpec((tm, w), lambda i: (i, 0))
    outs = [(d, F32), (FOX_WIDTH, BF16), (FOX_WIDTH, F32), (FOX_WIDTH, F32), (FOX_WIDTH, F32),
            (LANES, F32), (RW_PAD, F32), (C_END - C_RET, F32)]
    out_specs = [row(w) for w, _ in outs]
    out_shape = [jax.ShapeDtypeStruct((n, w), dt) for w, dt in outs]
    if tiles_per_seq:
        tps = tiles_per_seq
        for j in (2, 3):
            out_specs[j] = pl.BlockSpec((1, FOX_WIDTH, tm), lambda i: (i // tps, 0, i % tps))
            out_shape[j] = jax.ShapeDtypeStruct((n // (tps * tm), FOX_WIDTH, tps * tm), F32)
    return pl.pallas_call(
        functools.partial(_dense_in_kernel, kv_transposed=bool(tiles_per_seq)),
        grid=(n // tm,),
        in_specs=[row(d), _resident(n1.shape), _resident(wg.shape), _resident(wu.shape), _resident(wd.shape),
                  _resident(nm.shape), _resident(win.shape), _resident(wkvt.shape), _resident(bfp.shape)],
        out_specs=out_specs,
        out_shape=out_shape,
        compiler_params=_cparams(("parallel",)),
        name="dense_in",
    )(x, n1, wg, wu, wd, nm, win, wkvt, bfp)


def _cumsum_kernel(lf_ref, lft_ref, ccol_ref, crow_ref, *, t_len):
    nh = lft_ref.shape[1]
    r = lax.broadcasted_iota(jnp.int32, (CUM_BLK, CUM_BLK), 0)
    c = lax.broadcasted_iota(jnp.int32, (CUM_BLK, CUM_BLK), 1)
    tri_l = (r >= c).astype(F32)
    tri_u = (r <= c).astype(F32)
    carry_c = jnp.zeros((1, LANES), F32)
    carry_r = jnp.zeros((nh, 1), F32)
    for s in range(0, t_len, CUM_BLK):
        n = min(CUM_BLK, t_len - s)
        x = lf_ref[0, s:s + n, :]
        hi = x.astype(BF16).astype(F32)
        r1 = x - hi
        mid = r1.astype(BF16).astype(F32)
        lo = r1 - mid
        tl = tri_l[:n, :n]
        cc = (jnp.dot(tl, hi, preferred_element_type=F32) + jnp.dot(tl, mid, preferred_element_type=F32)
              + jnp.dot(tl, lo, preferred_element_type=F32)) + carry_c
        ccol_ref[0, s:s + n, :] = cc
        carry_c = cc[n - 1:n, :]
        y = lft_ref[0, :, s:s + n]
        hi = y.astype(BF16).astype(F32)
        r1 = y - hi
        mid = r1.astype(BF16).astype(F32)
        lo = r1 - mid
        tu = tri_u[:n, :n]
        cr = (jnp.dot(hi, tu, preferred_element_type=F32) + jnp.dot(mid, tu, preferred_element_type=F32)
              + jnp.dot(lo, tu, preferred_element_type=F32)) + carry_r
        crow_ref[0, :, s:s + n] = cr
        carry_r = cr[:, n - 1:n]


def _cumsum(lf3, lft):
    b, t, _ = lf3.shape
    nh = lft.shape[1]
    return pl.pallas_call(
        functools.partial(_cumsum_kernel, t_len=t),
        grid=(b,),
        in_specs=[pl.BlockSpec((1, t, LANES), lambda i: (i, 0, 0)), pl.BlockSpec((1, nh, t), lambda i: (i, 0, 0))],
        out_specs=[pl.BlockSpec((1, t, LANES), lambda i: (i, 0, 0)), pl.BlockSpec((1, nh, t), lambda i: (i, 0, 0))],
        out_shape=[jax.ShapeDtypeStruct((b, t, LANES), F32), jax.ShapeDtypeStruct((b, nh, t), F32)],
        compiler_params=_cparams(("parallel",)),
        name="logf_cumsum",
    )(lf3, lft)


def _qk(q, k):
    return lax.dot_general(q, k, (((1,), (1,)), ((), ())), preferred_element_type=F32)


def _softmax_pv(s_list, v_list, v_transposed):
    m = None
    for s in s_list:
        mi = jnp.max(s, axis=1, keepdims=True)
        m = mi if m is None else jnp.maximum(m, mi)
    l = None
    o = None
    for s, v, vt in zip(s_list, v_list, v_transposed):
        p = jnp.exp(s - m)
        li = jnp.sum(p, axis=1, keepdims=True)
        oi = _qk(p.astype(BF16), v) if vt else jnp.dot(p.astype(BF16), v, preferred_element_type=F32)
        l = li if l is None else l + li
        o = oi if o is None else o + oi
    return o / l


def _fox_prompt_kernel(q_ref, k_ref, v_ref, g_ref, ccol_ref, crow_ref, o_ref, *, t_len):
    pair = pl.program_id(1)
    lane = lax.broadcasted_iota(jnp.int32, (1, LANES), 1)
    low = lane < HEAD_DIM
    kb = k_ref[0].astype(BF16)
    vb = v_ref[0].astype(BF16)
    rr = lax.broadcasted_iota(jnp.int32, (FOX_BQ, FOX_BQ), 0)
    cc = lax.broadcasted_iota(jnp.int32, (FOX_BQ, FOX_BQ), 1)
    causal = rr >= cc
    for s0 in range(0, t_len, FOX_BQ):
        e = s0 + FOX_BQ
        q = q_ref[0, s0:e, :]
        ccol = ccol_ref[0, s0:e, :]
        zq = jnp.zeros_like(q)
        q2 = jnp.concatenate([jnp.where(low, q, zq), jnp.where(low, zq, q)], axis=0)
        sd2 = jnp.dot(q2, kb[:, s0:e], preferred_element_type=F32)
        sp2 = jnp.dot(q2, kb[:, :s0], preferred_element_type=F32) if s0 > 0 else None
        p_diag, p_past, denom = [], [], []
        for hh in range(2):
            h = 2 * pair + hh
            rows = slice(hh * FOX_BQ, (hh + 1) * FOX_BQ)
            cq = jnp.sum(jnp.where(lane == h, ccol, 0.0), axis=1, keepdims=True)
            ck = crow_ref[0, pl.ds(h, 1), :]
            sd = jnp.where(causal, sd2[rows] + cq - ck[:, s0:e], -jnp.inf)
            m = jnp.max(sd, axis=1, keepdims=True)
            if s0 > 0:
                sp = sp2[rows] + cq - ck[:, :s0]
                m = jnp.maximum(m, jnp.max(sp, axis=1, keepdims=True))
                pp = jnp.exp(sp - m)
                p_past.append(pp.astype(BF16))
            pd = jnp.exp(sd - m)
            p_diag.append(pd.astype(BF16))
            l = jnp.sum(pd, axis=1, keepdims=True)
            denom.append(l + jnp.sum(pp, axis=1, keepdims=True) if s0 > 0 else l)
        o2 = _qk(jnp.concatenate(p_diag, axis=0), vb[:, s0:e])
        if s0 > 0:
            o2 = o2 + _qk(jnp.concatenate(p_past, axis=0), vb[:, :s0])
        o = jnp.where(low, o2[:FOX_BQ] / denom[0], o2[FOX_BQ:] / denom[1]) * _sigmoid(g_ref[0, s0:e, :])
        o_ref[0, s0:e, :] = o.astype(o_ref.dtype)


def _fox_prompt(q3, kt, vt, g3, ccol, crow):
    b, t, w = q3.shape
    npair = w // LANES
    nh = crow.shape[1]
    blk = lambda: pl.BlockSpec((1, t, LANES), lambda i, p: (i, 0, p))
    blk_t = lambda: pl.BlockSpec((1, LANES, t), lambda i, p: (i, p, 0))
    return pl.pallas_call(
        functools.partial(_fox_prompt_kernel, t_len=t),
        grid=(b, npair),
        in_specs=[blk(), blk_t(), blk_t(), blk(),
                  pl.BlockSpec((1, t, LANES), lambda i, p: (i, 0, 0)),
                  pl.BlockSpec((1, nh, t), lambda i, p: (i, 0, 0))],
        out_specs=blk(),
        out_shape=jax.ShapeDtypeStruct((b, t, w), BF16),
        compiler_params=_cparams(("parallel", "parallel")),
        name="fox_prompt",
    )(q3, kt, vt, g3, ccol, crow)


def _fox_sample_kernel(q_ref, kn_ref, vn_ref, g_ref, kc_ref, vc_ref, ccol_ref, crow_ref, o_ref, *, past, n_new):
    pair = pl.program_id(1)
    lane = lax.broadcasted_iota(jnp.int32, (1, LANES), 1)
    low = lane < HEAD_DIM
    kc = kc_ref[0, 0].reshape(LANES, past).astype(BF16)
    vc = vc_ref[0, 0].reshape(LANES, past).astype(BF16)
    kn = kn_ref[0].astype(BF16)
    vn = vn_ref[0].astype(BF16)
    q = q_ref[0]
    ccol = ccol_ref[0, past:past + n_new, :]
    rr = lax.broadcasted_iota(jnp.int32, (n_new, n_new), 0)
    cc = lax.broadcasted_iota(jnp.int32, (n_new, n_new), 1)
    causal = rr >= cc
    res = []
    for hh in range(2):
        h = 2 * pair + hh
        msk = low if hh == 0 else jnp.logical_not(low)
        qm = jnp.where(msk, q, jnp.zeros_like(q))
        cq = jnp.sum(jnp.where(lane == h, ccol, 0.0), axis=1, keepdims=True)
        ck = crow_ref[0, pl.ds(h, 1), :]
        sc = jnp.dot(qm, kc, preferred_element_type=F32) + cq - ck[:, :past]
        sn = _qk(qm, kn) + cq - ck[:, past:past + n_new]
        sn = jnp.where(causal, sn, -jnp.inf)
        res.append(_softmax_pv([sc, sn], [vc, vn], [True, False]))
    o = jnp.where(low, res[0], res[1]) * _sigmoid(g_ref[0])
    o_ref[0] = o.astype(o_ref.dtype)


def _fox_sample(q3, k3, v3, g3, kc, vc, layer, ccol, crow):
    b, n_new, w = q3.shape
    past = kc.shape[4]
    npair = w // LANES
    nh = crow.shape[1]
    tot = past + n_new
    blk = lambda: pl.BlockSpec((1, n_new, LANES), lambda i, p: (i, 0, p))
    cache = lambda: pl.BlockSpec((1, 1, 2, HEAD_DIM, past), lambda i, p: (layer, i, p, 0, 0))
    return pl.pallas_call(
        functools.partial(_fox_sample_kernel, past=past, n_new=n_new),
        grid=(b, npair),
        in_specs=[blk(), blk(), blk(), blk(), cache(), cache(),
                  pl.BlockSpec((1, tot, LANES), lambda i, p: (i, 0, 0)),
                  pl.BlockSpec((1, nh, tot), lambda i, p: (i, 0, 0))],
        out_specs=blk(),
        out_shape=jax.ShapeDtypeStruct((b, n_new, w), BF16),
        compiler_params=_cparams(("parallel", "parallel")),
        name="fox_sample",
    )(q3, k3, v3, g3, kc, vc, ccol, crow)


def _rwkv_pre_kernel(z_ref, zp_ref, prev_ref, mu_ref, w0_ref, a0_ref, kk_ref, ka_ref, rk_ref,
                     wup_ref, aup_ref, gup_ref, bd_ref, scan_ref, post_ref):
    i = pl.program_id(1)
    z = z_ref[0]
    tt = z.shape[0]
    p_in = jnp.where(i == 0, prev_ref[0], zp_ref[0, 7:8, :])
    row = lax.broadcasted_iota(jnp.int32, (tt, 1), 0)
    shifted = jnp.where(row == 0, p_in, pltpu.roll(z, 1, axis=0))
    zs = z + mu_ref[...] * (shifted - z)
    w_ = RWKV_WIDTH
    r, k, v, low = zs[:, 0:w_], zs[:, w_:2 * w_], zs[:, 2 * w_:3 * w_], zs[:, 3 * w_:]
    bd = bd_ref[...]
    y = w0_ref[...] + jnp.dot(jnp.tanh(low).astype(BF16), wup_ref[...], preferred_element_type=F32)
    sp = jnp.maximum(-y, 0.0) + _log1pexp_neg_abs(y)
    decay = jnp.exp(-jnp.exp(-sp - 0.5))
    a = _sigmoid(a0_ref[...] + jnp.dot(low.astype(BF16), aup_ref[...], preferred_element_type=F32))
    g = jnp.dot(_sigmoid(low).astype(BF16), gup_ref[...], preferred_element_type=F32)
    kk = k * kk_ref[...]
    kk = kk * lax.rsqrt(jnp.maximum(_seg_sum(kk * kk, bd), 1e-24))
    k2 = k * (1.0 + (a - 1.0) * ka_ref[...])
    bonus = _seg_sum(r * k2 * rk_ref[...], bd) * v
    fields = {F_R: r, F_W: decay, F_K: k2, F_V: v, F_KK: kk, F_KKA: kk * a}
    per_head = 2 * SCAN_PAIRS * HEAD_DIM
    for h in range(RWKV_HEADS):
        for f, val in fields.items():
            c0 = h * per_head + f * HEAD_DIM
            scan_ref[:, c0:c0 + HEAD_DIM] = val[:, h * HEAD_DIM:(h + 1) * HEAD_DIM]
    post_ref[0, :, 0:w_] = g
    post_ref[0, :, w_:2 * w_] = bonus


def _rwkv_pre(zr3, prev, mu, w0, a0, k_k, k_a, r_k, wup, aup, gup, bd):
    b, t, w = zr3.shape
    tt = min(RWKV_TT, t)
    full = lambda a: pl.BlockSpec(a.shape, lambda i, j: (0,) * a.ndim)
    return pl.pallas_call(
        _rwkv_pre_kernel,
        grid=(b, t // tt),
        in_specs=[pl.BlockSpec((1, tt, w), lambda i, j: (i, j, 0)),
                  pl.BlockSpec((1, 8, w), lambda i, j: (i, jnp.maximum(j * (tt // 8) - 1, 0), 0)),
                  pl.BlockSpec((1, 1, w), lambda i, j: (i, 0, 0)),
                  full(mu), full(w0), full(a0), full(k_k), full(k_a), full(r_k), full(wup), full(aup), full(gup),
                  full(bd)],
        out_specs=[pl.BlockSpec((tt, 6 * RWKV_WIDTH), lambda i, j: (j, i)),
                   pl.BlockSpec((1, tt, 2 * RWKV_WIDTH), lambda i, j: (i, j, 0))],
        out_shape=[jax.ShapeDtypeStruct((t, b * 6 * RWKV_WIDTH), F32), jax.ShapeDtypeStruct((b, t, 2 * RWKV_WIDTH), F32)],
        compiler_params=_cparams(("parallel", "parallel")),
        name="rwkv_pre",
    )(zr3, zr3, prev, mu, w0, a0, k_k, k_a, r_k, wup, aup, gup, bd)


def _rwkv_scan_kernel(*refs):
    x_ref, s0_ref, _, o_ref, s_ref, x_scr, o_scr = refs
    i = pl.program_id(0)
    tc, nb = x_ref.shape[0], x_ref.shape[1]
    ng = HEAD_DIM // 8

    @pl.when(i == 0)
    def _():
        s_ref[...] = s0_ref[...]

    def relayout(t):
        for fp in range(SCAN_PAIRS):
            cols = [(h * SCAN_PAIRS + fp) * LANES for h in range(RWKV_HEADS)]
            y = jnp.concatenate([x_ref[t, :, c:c + LANES] for c in cols], axis=0)
            yt = y.T
            x_scr[t, 2 * fp] = yt[0:HEAD_DIM]
            x_scr[t, 2 * fp + 1] = yt[HEAD_DIM:2 * HEAD_DIM]

    relayout(0)

    def step(t, carry):
        relayout(jnp.minimum(t + 1, tc - 1))
        def dot_kk(k, acc):
            kk = x_scr[t, F_KK, pl.ds(k, 1), :]
            return tuple(acc[g] + s_ref[k, g * 8:(g + 1) * 8, :] * kk for g in range(ng))

        zero = tuple(jnp.zeros((8, LANES), F32) for _ in range(ng))
        skk = lax.fori_loop(0, HEAD_DIM, dot_kk, zero, unroll=SCAN_K_UNROLL)
        vv = tuple(x_scr[t, F_V, g * 8:(g + 1) * 8, :] for g in range(ng))

        def update(k, acc):
            w = x_scr[t, F_W, pl.ds(k, 1), :]
            kka = x_scr[t, F_KKA, pl.ds(k, 1), :]
            kx = x_scr[t, F_K, pl.ds(k, 1), :]
            r = x_scr[t, F_R, pl.ds(k, 1), :]
            out = []
            for g in range(ng):
                sn = s_ref[k, g * 8:(g + 1) * 8, :] * w - skk[g] * kka + vv[g] * kx
                s_ref[k, g * 8:(g + 1) * 8, :] = sn
                out.append(acc[g] + sn * r)
            return tuple(out)

        o = lax.fori_loop(0, HEAD_DIM, update, zero, unroll=SCAN_K_UNROLL)
        for g in range(ng):
            o_scr[t, g * 8:(g + 1) * 8, :] = o[g]
        return carry

    lax.fori_loop(0, tc, step, 0)

    low = lax.broadcasted_iota(jnp.int32, (1, LANES), 1) < HEAD_DIM
    for t in range(0, tc, 2):
        z = jnp.concatenate([o_scr[t], o_scr[t + 1]], axis=0).T
        zr = pltpu.roll(z, HEAD_DIM, axis=1)
        for p in range(RWKV_HEADS // 2):
            ev = slice(2 * p * nb, (2 * p + 1) * nb)
            od = slice((2 * p + 1) * nb, (2 * p + 2) * nb)
            o_ref[t, :, p * LANES:(p + 1) * LANES] = jnp.where(low, z[ev], zr[od])
            o_ref[t + 1, :, p * LANES:(p + 1) * LANES] = jnp.where(low, zr[ev], z[od])


def _rwkv_scan(scan_in, s0, after):
    t, b, w = scan_in.shape
    tc = min(SCAN_TC, t)
    state = pl.BlockSpec((HEAD_DIM, HEAD_DIM, LANES), lambda i: (0, 0, 0))
    return pl.pallas_call(
        _rwkv_scan_kernel,
        grid=(t // tc,),
        in_specs=[pl.BlockSpec((tc, b, w), lambda i: (i, 0, 0)), state,
                  pl.BlockSpec((1, 16, LANES), lambda i: (0, 0, 0))],
        out_specs=[pl.BlockSpec((tc, b, RWKV_WIDTH), lambda i: (i, 0, 0)), state],
        out_shape=[jax.ShapeDtypeStruct((t, b, RWKV_WIDTH), F32),
                   jax.ShapeDtypeStruct((HEAD_DIM, HEAD_DIM, LANES), F32)],
        scratch_shapes=[pltpu.VMEM((tc, 2 * SCAN_PAIRS, HEAD_DIM, LANES), F32), pltpu.VMEM((tc, HEAD_DIM, LANES), F32)],
        compiler_params=_cparams(("arbitrary",)),
        name="rwkv_scan",
    )(scan_in, s0, after)


def _retention_kernel(z_ref, cos_ref, sin_ref, d_ref, dq_ref, ds_ref, gc_ref, r0_ref, bd_ref,
                      o_ref, r_scr):
    i = pl.program_id(1)

    @pl.when(i == 0)
    def _():
        r_scr[...] = r0_ref[...]

    cos = cos_ref[...]
    sin = sin_ref[...]
    lane = lax.broadcasted_iota(jnp.int32, (1, LANES), 1)
    low = lane < HEAD_DIM
    first_half = (lane % HEAD_DIM) < (HEAD_DIM // 2)
    rr = lax.broadcasted_iota(jnp.int32, (LANES, LANES), 0)
    cc = lax.broadcasted_iota(jnp.int32, (LANES, LANES), 1)
    same_head = (rr < HEAD_DIM) == (cc < HEAD_DIM)
    bd = bd_ref[...]
    w_ = RET_WIDTH

    def rot(x):
        swapped = jnp.where(first_half, pltpu.roll(x, LANES - HEAD_DIM // 2, axis=1), pltpu.roll(x, HEAD_DIM // 2, axis=1))
        return x * cos + swapped * sin

    for p in range(RET_HEADS // 2):
        c0 = p * LANES
        qr = rot(z_ref[0, :, c0:c0 + LANES])
        kr = rot(z_ref[0, :, w_ + c0:w_ + c0 + LANES]) * (HEAD_DIM ** -0.5)
        vb = z_ref[0, :, 2 * w_ + c0:2 * w_ + c0 + LANES].astype(BF16)
        gate = z_ref[0, :, 3 * w_ + c0:3 * w_ + c0 + LANES]
        qb = qr.astype(BF16)
        kb = kr.astype(BF16)
        r_old = r_scr[0, p]
        o = jnp.dot((qr * dq_ref[p]).astype(BF16), r_old.astype(BF16), preferred_element_type=F32)
        for hh in range(2):
            msk = low if hh == 0 else jnp.logical_not(low)
            qm = jnp.where(msk, qb, jnp.zeros_like(qb))
            pmat = (_qk(qm, kb) * d_ref[2 * p + hh]).astype(BF16)
            o = o + jnp.where(msk, jnp.dot(pmat, vb, preferred_element_type=F32), 0.0)
        kd = (kr * ds_ref[p]).astype(BF16)
        kv = lax.dot_general(kd, vb, (((0,), (0,)), ((), ())), preferred_element_type=F32)
        r_scr[0, p] = r_old * gc_ref[p] + jnp.where(same_head, kv, 0.0)
        out = _head_norm(o, bd, RET_GN_EPS) * (gate * _sigmoid(gate))
        o_ref[0, :, c0:c0 + LANES] = out.astype(o_ref.dtype)


def _retention(zt3, cos, sin, dmat, dq, ds, gc, r0, bd, blk):
    b, t, w = zt3.shape
    npair = RET_HEADS // 2
    full = lambda a: pl.BlockSpec(a.shape, lambda i, j: (0,) * a.ndim)
    return pl.pallas_call(
        _retention_kernel,
        grid=(b, t // blk),
        in_specs=[pl.BlockSpec((1, blk, w), lambda i, j: (i, j, 0)),
                  pl.BlockSpec((blk, LANES), lambda i, j: (j, 0)),
                  pl.BlockSpec((blk, LANES), lambda i, j: (j, 0)),
                  full(dmat), full(dq), full(ds), full(gc),
                  pl.BlockSpec((1, npair, LANES, LANES), lambda i, j: (i, 0, 0, 0)),
                  full(bd)],
        out_specs=[pl.BlockSpec((1, blk, RET_WIDTH), lambda i, j: (i, j, 0)),
                   pl.BlockSpec((1, npair, LANES, LANES), lambda i, j: (i, 0, 0, 0))],
        out_shape=[jax.ShapeDtypeStruct((b, t, RET_WIDTH), BF16),
                   jax.ShapeDtypeStruct((b, npair, LANES, LANES), F32)],
        compiler_params=_cparams(("parallel", "arbitrary")),
        name="retention",
    )(zt3, cos, sin, dmat, dq, ds, gc, r0, bd)


def _retention_tables(blk, chunk, t_len, pos0):
    log_g = jnp.log(1.0 - jnp.power(2.0, -5.0 - jnp.arange(RET_HEADS, dtype=F32)))
    idx = jnp.arange(blk)
    ci = idx // chunk
    diff = (idx[:, None] - idx[None, :]).astype(F32)
    expo = jnp.where(ci[:, None] == ci[None, :], jnp.abs(diff), diff)
    dmat = jnp.exp(log_g[:, None, None] * expo[None])
    dmat = jnp.where((ci[:, None] >= ci[None, :])[None], dmat, 0.0)
    lg_lane = jnp.repeat(log_g, HEAD_DIM).reshape(RET_HEADS // 2, 1, LANES)
    tau = idx.astype(F32)[None, :, None]
    dq = jnp.exp(lg_lane * (tau + 1.0))
    ds = jnp.exp(lg_lane * (blk - 1.0 - tau))
    gc = jnp.exp(lg_lane * float(blk))
    half = HEAD_DIM // 2
    inv = 1.0 / (ROPE_BASE ** (jnp.arange(half, dtype=F32) / half))
    ang = (jnp.arange(t_len) + pos0).astype(F32)[:, None] * inv[None, :]
    cos = jnp.tile(jnp.cos(ang), (1, LANES // half))
    sin = jnp.sin(ang)
    sin = jnp.tile(jnp.concatenate([-sin, sin], axis=1), (1, LANES // HEAD_DIM))
    return cos, sin, dmat, dq, ds, gc


def _dense_out_kernel(x_ref, of_ref, orw_ref, gb_ref, oret_ref, pe_ref, lnw_ref, lnb_ref, bd_ref,
                      wo1_ref, wo2_ref, wo3_ref, n2_ref, wg_ref, wu_ref, wd_ref, wpp_ref, pn_ref, wpg_ref, nf_ref,
                      xo_ref, *, final):
    w_ = RWKV_WIDTH
    hn = _head_norm(orw_ref[...], bd_ref[...], RWKV_GN_EPS)
    orw = ((hn * lnw_ref[...] + lnb_ref[...] + gb_ref[:, w_:2 * w_]) * gb_ref[:, 0:w_]).astype(BF16)
    x = x_ref[...]
    x = x + (jnp.dot(of_ref[...], wo1_ref[...], preferred_element_type=F32)
             + jnp.dot(orw, wo2_ref[...], preferred_element_type=F32)
             + jnp.dot(oret_ref[...], wo3_ref[...], preferred_element_type=F32))
    x = x + 0.5 * _swiglu(_rms(x, n2_ref[...]).astype(BF16), wg_ref, wu_ref, wd_ref)
    e = _rms(jnp.dot(pe_ref[...].astype(BF16), wpp_ref[...], preferred_element_type=F32), pn_ref[...])
    x = x + _sigmoid(jnp.dot(x.astype(BF16), wpg_ref[...], preferred_element_type=F32)) * e
    if final:
        x = _rms(x, nf_ref[...])
    xo_ref[...] = x


def _dense_out(x, of, orw, gb, oret, pe, lnw, lnb, bd, wo1, wo2, wo3, n2, wg, wu, wd, wpp, pn, wpg, nf, final,
               orw_tiles_per_seq, layer):
    n, d = x.shape
    tm = DENSE_OUT_TM
    row = lambda a: pl.BlockSpec((tm, a.shape[1]), lambda i: (i, 0))
    pe_spec = pl.BlockSpec((None, tm, pe.shape[2]), lambda i: (layer, i, 0))
    if orw_tiles_per_seq:
        tps = orw_tiles_per_seq
        orw_spec = pl.BlockSpec((tm, RWKV_WIDTH), lambda i: (i % tps, i // tps))
    else:
        orw_spec = row(orw)
    consts = (lnw, lnb, bd, wo1, wo2, wo3, n2, wg, wu, wd, wpp, pn, wpg, nf)
    return pl.pallas_call(
        functools.partial(_dense_out_kernel, final=final),
        grid=(n // tm,),
        in_specs=[row(x), row(of), orw_spec, row(gb), row(oret), pe_spec] + [_resident(c.shape) for c in consts],
        out_specs=row(x),
        out_shape=jax.ShapeDtypeStruct((n, d), F32),
        compiler_params=_cparams(("parallel",)),
        name="dense_out",
    )(x, of, orw, gb, oret, pe, *consts)


def _block_diag_ones(width):
    idx = np.arange(width) // HEAD_DIM
    return jnp.asarray(idx[:, None] == idx[None, :], F32)


def _group(x, pe_all, hist, lw, tables, layer, final):
    (n1, wg1, wu1, wd1, nm, win, wkvt, bfp, mu, w0, a0, k_k, k_a, r_k, wup, aup, gup, lnw, lnb,
     wo1, wo2, wo3, n2, wg2, wu2, wd2, wpp, pn, wpg, nf) = lw
    b, t, d = x.shape
    n = b * t
    bd256, bd128 = _block_diag_ones(RWKV_WIDTH), _block_diag_ones(LANES)
    tiles_per_seq = t // DENSE_TM if t % DENSE_TM == 0 else 0
    x2, q, k, v, fg, lf, zr, zt = _dense_in(x.reshape(n, d), n1, wg1, wu1, wd1, nm, win, wkvt, bfp, tiles_per_seq)
    r3 = lambda a: a.reshape(b, t, a.shape[-1])
    lf3 = r3(lf)
    logf = lf3[:, :, :FOX_HEADS]
    zr3 = r3(zr)
    if hist is None:
        prev = jnp.zeros((b, 1, RW_PAD), F32)
        s0 = jnp.zeros((HEAD_DIM, HEAD_DIM, RWKV_HEADS * b), F32)
        r0 = jnp.zeros((b, RET_HEADS // 2, LANES, LANES), F32)
        blk = RET_BLOCK
    else:
        kc, vc, clf, s0, prev, r0 = hist
        blk = t
    cos, sin, dmat, dq, ds, gc = tables
    scan_in, post_in = _rwkv_pre(zr3, prev, mu, w0, a0, k_k, k_a, r_k, wup, aup, gup, bd256)
    if hist is None:
        lft = logf.transpose(0, 2, 1)
        ccol, crow = _cumsum(lf3, lft)
        assert tiles_per_seq, "prompt length must be a multiple of the dense tile"
        of = _fox_prompt(r3(q), k, v, r3(fg), ccol, crow)
        fk, fv = (a.reshape(b, FOX_HEADS, HEAD_DIM, t) for a in (k, v))
    else:
        lf_all = jnp.concatenate([jnp.pad(clf, ((0, 0), (0, 0), (0, LANES - FOX_HEADS))), lf3], axis=1)
        lft = lf_all[:, :, :FOX_HEADS].transpose(0, 2, 1)
        ccol, crow = _cumsum(lf_all, lft)
        assert not tiles_per_seq
        of = _fox_sample(r3(q), r3(k), r3(v), r3(fg), kc, vc, layer, ccol, crow)
        fk, fv = (a.reshape(b, t, FOX_HEADS, HEAD_DIM) for a in (k, v))
    oret, r_fin = _retention(r3(zt), cos, sin, dmat, dq, ds, gc, r0, bd128, blk)
    o_tm, s_fin = _rwkv_scan(scan_in.reshape(t, b, 6 * RWKV_WIDTH), s0, of)
    out_tiles_per_seq = t // DENSE_OUT_TM if t % DENSE_OUT_TM == 0 else 0
    if out_tiles_per_seq:
        orw = o_tm.reshape(t, b * RWKV_WIDTH)
    else:
        orw = o_tm.transpose(1, 0, 2).reshape(n, RWKV_WIDTH)
    xo = _dense_out(x2, of.reshape(n, FOX_WIDTH), orw, post_in.reshape(n, 2 * RWKV_WIDTH), oret.reshape(n, RET_WIDTH),
                    pe_all.reshape(pe_all.shape[0], n, pe_all.shape[-1]), lnw, lnb, bd256, wo1, wo2, wo3, n2, wg2, wu2,
                    wd2, wpp, pn, wpg, nf, final, out_tiles_per_seq, layer)
    s_new = s_fin.reshape(HEAD_DIM, HEAD_DIM, RWKV_HEADS, b).transpose(3, 2, 1, 0)
    r_new = jnp.stack([r_fin[:, hh // 2, (hh % 2) * HEAD_DIM:(hh % 2 + 1) * HEAD_DIM,
                             (hh % 2) * HEAD_DIM:(hh % 2 + 1) * HEAD_DIM] for hh in range(RET_HEADS)], axis=1)
    state = (fk, fv, logf, s_new, zr3[:, t - 1:t, :], r_new)
    return xo.reshape(b, t, d), state


def kernel(x_prompt, x_sample, cache_fox_k, cache_fox_v, cache_fox_logf, state_rwkv, state_rwkv_shift, state_ret,
           p_prompt, p_sample, norm_ffn1, w_ffn1_gate, w_ffn1_up, w_ffn1_down, norm_mix, w_in, fox_forget_bias,
           rwkv_mu, rwkv_w0, rwkv_w_up, rwkv_a0, rwkv_a_up, rwkv_g_up, rwkv_k_k, rwkv_k_a, rwkv_r_k, rwkv_ln_w,
           rwkv_ln_b, w_out, norm_ffn2, w_ffn2_gate, w_ffn2_up, w_ffn2_down, w_ple_proj, ple_norm, w_ple_gate,
           norm_final):
    depth, d, _ = w_in.shape
    bp, tp, _ = x_prompt.shape
    bs, ts, _ = x_sample.shape
    past = cache_fox_k.shape[2]
    assert RWKV_HEADS * bp == LANES and RWKV_HEADS * bs == LANES, "rwkv scan puts (head, batch) on the 128 lanes"
    assert tp % RET_BLOCK == 0 and tp % FOX_BQ == 0 and (bp * tp) % DENSE_OUT_TM == 0 and (bs * ts) % DENSE_OUT_TM == 0

    fw, rw = FOX_WIDTH, RWKV_WIDTH
    o_f = 3 * fw
    o_g = o_f + FOX_HEADS
    o_rw = o_g + fw
    o_ret = o_rw + 3 * rw + 3 * RWKV_RANK
    zpad = lambda w: jnp.zeros((depth, d, w), F32)
    r_ = RWKV_RANK
    rw_cols = 3 * rw + 3 * r_

    def regroup(a):
        parts = [a[..., 0:rw], a[..., rw + r_:2 * rw + r_], a[..., 2 * rw + r_:3 * rw + r_], a[..., rw:rw + r_],
                 a[..., 3 * rw + r_:rw_cols], jnp.zeros(a.shape[:-1] + (RW_PAD - rw_cols,), a.dtype)]
        return jnp.concatenate(parts, axis=-1)

    def ungroup(a):
        parts = [a[..., 0:rw], a[..., 3 * rw:3 * rw + r_], a[..., rw:3 * rw], a[..., 3 * rw + r_:rw_cols]]
        return jnp.concatenate(parts, axis=-1)

    win_all = jnp.concatenate([w_in[:, :, 0:o_f], w_in[:, :, o_g:o_rw], w_in[:, :, o_f:o_g], zpad(LANES - FOX_HEADS),
                               regroup(w_in[:, :, o_rw:o_ret]), w_in[:, :, o_ret:]], axis=2).astype(BF16)
    wkvt = win_all[:, :, C_K:C_G].transpose(0, 2, 1)
    bfp = jnp.pad(fox_forget_bias, ((0, 0), (0, LANES - FOX_HEADS)))[:, None, :]
    mu_p = regroup(rwkv_mu)[:, None, :]
    lowrank = lambda w, slot: jnp.pad(w, ((0, 0), (slot * RWKV_RANK, LANES - (slot + 1) * RWKV_RANK), (0, 0))).astype(BF16)
    wup, aup, gup = lowrank(rwkv_w_up, 0), lowrank(rwkv_a_up, 1), lowrank(rwkv_g_up, 2)
    vec = lambda a: a.reshape(depth, 1, -1)
    bf = lambda a: a.astype(BF16)
    wg1, wu1, wd1 = bf(w_ffn1_gate), bf(w_ffn1_up), bf(w_ffn1_down)
    wg2, wu2, wd2 = bf(w_ffn2_gate), bf(w_ffn2_up), bf(w_ffn2_down)
    wo = bf(w_out)
    wpp, wpg = bf(w_ple_proj), bf(w_ple_gate)
    nf = norm_final.reshape(1, -1)

    def layer_weights(l):
        return (vec(norm_ffn1)[l], wg1[l], wu1[l], wd1[l], vec(norm_mix)[l], win_all[l], wkvt[l], bfp[l], mu_p[l],
                vec(rwkv_w0)[l], vec(rwkv_a0)[l], vec(rwkv_k_k)[l], vec(rwkv_k_a)[l], vec(rwkv_r_k)[l],
                wup[l], aup[l], gup[l], vec(rwkv_ln_w)[l], vec(rwkv_ln_b)[l],
                wo[l, 0:fw], wo[l, fw:fw + rw], wo[l, fw + rw:], vec(norm_ffn2)[l], wg2[l], wu2[l], wd2[l],
                wpp[l], vec(ple_norm)[l], wpg[l], nf)

    kc = cache_fox_k.transpose(0, 1, 3, 4, 2)
    vc = cache_fox_v.transpose(0, 1, 3, 4, 2)
    s0_all = state_rwkv.transpose(0, 4, 3, 2, 1).reshape(depth, HEAD_DIM, HEAD_DIM, RWKV_HEADS * bs)
    prev_all = regroup(state_rwkv_shift)
    eye2 = jnp.eye(2, dtype=F32)
    r0_all = jnp.einsum('lbpqde,qr->lbpqdre', state_ret.reshape(depth, bs, RET_HEADS // 2, 2, HEAD_DIM, HEAD_DIM),
                        eye2).reshape(depth, bs, RET_HEADS // 2, LANES, LANES)

    tab_p = _retention_tables(RET_BLOCK, RET_CHUNK, tp, 0)
    tab_s = _retention_tables(ts, ts, ts, past)

    xp, xs = x_prompt, x_sample
    p_states, s_states = [], []
    for l in range(depth):
        lw = layer_weights(l)
        final = l == depth - 1
        xp, sp = _group(xp, p_prompt, None, lw, tab_p, l, final)
        hist = (kc, vc, cache_fox_logf[l], s0_all[l], prev_all[l], r0_all[l])
        xs, ss = _group(xs, p_sample, hist, lw, tab_s, l, final)
        p_states.append(sp)
        s_states.append(ss)

    def stacked(states, kv_transposed):
        cols = [jnp.stack([s[i] for s in states]) for i in range(6)]
        if kv_transposed:
            cols[0], cols[1] = cols[0].transpose(0, 1, 4, 2, 3), cols[1].transpose(0, 1, 4, 2, 3)
        cols[4] = ungroup(cols[4])
        return cols

    return (xp, xs, *stacked(p_states, True), *stacked(s_states, False))
```
